```python
import jax
import jax.numpy as jnp
from jax import lax
import numpy as np

D_MODEL = 1024
BATCH = 1
SEQ = 16384
DEPTH = 1
DEC_BATCH = 128
DEC_SEQ = 8
PAST_LEN = 16384
PAGE_SIZE = 128

NUM_META = 16
MLA_HEADS = 8
MLA_NOPE = 64
MLA_ROPE = 32
MLA_QK = MLA_NOPE + MLA_ROPE
MLA_V = 64
Q_LORA = 768
KV_LORA = 256
FOX_HEADS = 8
FOX_HD = 64
FORGET_BIAS_INIT = 2.0
MIX_WIDTH = MLA_HEADS * MLA_V + FOX_HEADS * FOX_HD
PEER_HEADS = 8
PEER_NKEYS = 128
PEER_EXPERTS = PEER_NKEYS * PEER_NKEYS
PEER_DK = 256
PEER_TOPK = 16
Q_BLOCK = 128
TOKEN_BLOCK = 128
ROPE_THETA = 10000.0
EPS = 1e-6
NEG_INF = -1e30

_SPLIT_SIZES = (Q_LORA, KV_LORA, MLA_ROPE, FOX_HEADS * FOX_HD, FOX_HEADS * FOX_HD, FOX_HEADS * FOX_HD, FOX_HEADS)
IN_WIDTH = sum(_SPLIT_SIZES)
SPLIT_POINTS = tuple(int(s) for s in np.cumsum(_SPLIT_SIZES)[:-1])

kernel_name = 'hybrid_mla_fox_peer_step'


def rms_norm(x, g):
    x32 = x.astype(jnp.float32)
    y = x32 * lax.rsqrt(jnp.mean(x32 * x32, axis=-1, keepdims=True) + EPS)
    return (y * g.astype(jnp.float32)).astype(x.dtype)


def apply_rope_tail(x, pos):
    half = MLA_ROPE // 2
    inv = jnp.power(ROPE_THETA, -jnp.arange(half, dtype=jnp.float32) / half)
    ang = pos.astype(jnp.float32)[:, None] * inv[None, :]
    cos = jnp.cos(ang)[None, :, None, :]
    sin = jnp.sin(ang)[None, :, None, :]
    pe = x[..., MLA_NOPE:].astype(jnp.float32)
    p1, p2 = pe[..., :half], pe[..., half:]
    rot = jnp.concatenate([p1 * cos - p2 * sin, p2 * cos + p1 * sin], axis=-1).astype(x.dtype)
    return jnp.concatenate([x[..., :MLA_NOPE], rot], axis=-1)


def token_features(h, p):
    B, S, _ = h.shape
    proj = h @ p['w_in']
    c_q, c_kv, k_pe, fq, fk, fv, flog = jnp.split(proj, SPLIT_POINTS, axis=-1)
    c_q = rms_norm(c_q, p['g_q_lora'])
    c_kv = rms_norm(c_kv, p['g_kv_lora'])
    fq = rms_norm(fq.reshape(B, S, FOX_HEADS, FOX_HD), p['g_fox_qn'])
    fk = rms_norm(fk.reshape(B, S, FOX_HEADS, FOX_HD), p['g_fox_kn'])
    fv = fv.reshape(B, S, FOX_HEADS, FOX_HD)
    logf = jax.nn.log_sigmoid((flog + p['b_forget']).astype(jnp.float32))
    return c_q, c_kv, k_pe, fq, fk, fv, logf


def mla_queries(c_q, p, pos):
    B, S, _ = c_q.shape
    q = (c_q @ p['w_uq']).reshape(B, S, MLA_HEADS, MLA_QK)
    return apply_rope_tail(rms_norm(q, p['g_mla_qn']), pos)


def mla_keys_values(c_kv, k_pe, p, pos):
    B, S, _ = c_kv.shape
    kv = (c_kv @ p['w_ukv']).reshape(B, S, MLA_HEADS, MLA_NOPE + MLA_V)
    k_nope, v = kv[..., :MLA_NOPE], kv[..., MLA_NOPE:]
    k_pe_h = jnp.broadcast_to(k_pe[:, :, None, :], (B, S, MLA_HEADS, MLA_ROPE)).astype(k_nope.dtype)
    k = jnp.concatenate([k_nope, k_pe_h], axis=-1)
    return apply_rope_tail(rms_norm(k, p['g_mla_kn']), pos), v


def attend(q, k, v, q_pos, k_pos, cum_q=None, cum_k=None):
    scale = q.shape[-1] ** -0.5
    logits = jnp.einsum('bqhd,bkhd->bhqk', q, k.astype(q.dtype)).astype(jnp.float32) * scale
    if cum_q is not None:
        logits = logits + jnp.transpose(cum_q, (0, 2, 1))[..., :, None] - jnp.transpose(cum_k, (0, 2, 1))[..., None, :]
    mask = (k_pos[None, :] <= q_pos[:, None]) & (k_pos[None, :] >= 0)
    logits = jnp.where(mask, logits, NEG_INF)
    probs = jax.nn.softmax(logits, axis=-1).astype(v.dtype)
    return jnp.einsum('bhqk,bkhd->bqhd', probs, v)


def blocked_causal_attention(q, k, v, pos, cum=None):
    B, Tp, H, d = q.shape
    nb = Tp // Q_BLOCK
    qb = q.reshape(B, nb, Q_BLOCK, H, d).swapaxes(0, 1)
    pb = pos.reshape(nb, Q_BLOCK)
    cb = None if cum is None else cum.reshape(B, nb, Q_BLOCK, H).swapaxes(0, 1)

    def step(blk):
        q_blk, p_blk, c_blk = blk
        return attend(q_blk, k, v, p_blk, pos, c_blk, cum)

    out = lax.map(step, (qb, pb, cb))
    return out.swapaxes(0, 1).reshape(B, Tp, H, v.shape[-1])


def peer_block(h, p):
    n = h.shape[0]
    half = PEER_DK // 2
    n_keys = p['peer_keys_a'].shape[1]
    q = (h @ p['w_peer_query']).reshape(n, PEER_HEADS, PEER_DK)
    s_a = jnp.einsum('nhd,hkd->nhk', q[..., :half], p['peer_keys_a']).astype(jnp.float32)
    s_b = jnp.einsum('nhd,hkd->nhk', q[..., half:], p['peer_keys_b']).astype(jnp.float32)
    va, ia = lax.top_k(s_a, PEER_TOPK)
    vb, ib = lax.top_k(s_b, PEER_TOPK)
    cand = (va[..., :, None] + vb[..., None, :]).reshape(n, PEER_HEADS, PEER_TOPK * PEER_TOPK)
    cand_idx = (ia[..., :, None] * n_keys + ib[..., None, :]).reshape(n, PEER_HEADS, PEER_TOPK * PEER_TOPK)
    best, sel = lax.top_k(cand, PEER_TOPK)
    idx = jnp.take_along_axis(cand_idx, sel, axis=-1)
    gate = jax.nn.softmax(best, axis=-1)
    u = jnp.take(p['peer_u'], idx, axis=0)
    act = jax.nn.gelu(jnp.einsum('nhkd,nd->nhk', u, h).astype(jnp.float32), approximate=False)
    w = (gate * act).astype(h.dtype)
    return jnp.einsum('nhk,nhkd->nd', w, jnp.take(p['peer_v'], idx, axis=0))


def peer_ffn(h, p):
    n, d = h.shape
    pad = (-n) % TOKEN_BLOCK
    hb = jnp.pad(h, ((0, pad), (0, 0))).reshape(-1, TOKEN_BLOCK, d)
    out = lax.map(lambda blk: peer_block(blk, p), hb)
    return out.reshape(-1, d)[:n]


def merge_and_channel(x, o_mla, o_fox, p):
    B, S, D = x.shape
    mixed = jnp.concatenate([rms_norm(o_mla.reshape(B, S, -1), p['g_mla_out']),
                             rms_norm(o_fox.reshape(B, S, -1), p['g_fox_out'])], axis=-1)
    x = x + mixed @ p['w_out']
    h2 = rms_norm(x, p['g_norm2']).reshape(B * S, D)
    return x + peer_ffn(h2, p).reshape(B, S, D)


def prompt_layer(x, p):
    B, T, _ = x.shape
    P = (-T) % Q_BLOCK
    h = jnp.pad(rms_norm(x, p['g_norm1']), ((0, 0), (P, 0), (0, 0)))
    pos = jnp.arange(T + P, dtype=jnp.int32) - P
    c_q, c_kv, k_pe, fq, fk, fv, logf = token_features(h, p)
    q_m = mla_queries(c_q, p, pos)
    k_m, v_m = mla_keys_values(c_kv, k_pe, p, pos)
    o_m = blocked_causal_attention(q_m, k_m, v_m, pos)[:, P:]
    cum = jnp.cumsum(logf, axis=1)
    o_f = blocked_causal_attention(fq, fk, fv, pos, cum)[:, P:]
    y = merge_and_channel(x, o_m, o_f, p)
    rows = (c_kv[:, P:], k_pe[:, P:], fk[:, P:], fv[:, P:], logf[:, P:])
    return y, rows


def sample_layer(x, lat_pool, rope_pool, k_pool, v_pool, lf_pool, page_table, p):
    B, S, _ = x.shape
    past = page_table.shape[1] * lat_pool.shape[1]
    h = rms_norm(x, p['g_norm1'])
    q_pos = past + jnp.arange(S, dtype=jnp.int32)
    k_pos = jnp.arange(past + S, dtype=jnp.int32)
    c_q, c_kv, k_pe, fq, fk, fv, logf = token_features(h, p)
    q_m = mla_queries(c_q, p, q_pos)

    def one_seq(args):
        pt, ckv1, kpe1, qm1, fq1, fk1, fv1, lf1 = args

        def gather(pool, new):
            old = pool[pt].reshape((past,) + pool.shape[2:])
            return jnp.concatenate([old, new.astype(old.dtype)], axis=0)

        lat = gather(lat_pool, ckv1)[None]
        kpe = gather(rope_pool, kpe1)[None]
        k_m, v_m = mla_keys_values(lat, kpe, p, k_pos)
        o_m = attend(qm1[None], k_m, v_m, q_pos, k_pos)
        keys = gather(k_pool, fk1)[None]
        vals = gather(v_pool, fv1)[None]
        cum = jnp.cumsum(gather(lf_pool, lf1).astype(jnp.float32), axis=0)[None]
        o_f = attend(fq1[None], keys, vals, q_pos, k_pos, cum[:, past:], cum)
        return o_m[0], o_f[0]

    o_m, o_f = lax.map(one_seq, (page_table, c_kv, k_pe, q_m, fq, fk, fv, logf))
    y = merge_and_channel(x, o_m, o_f, p)
    rows = (c_kv, k_pe, fk, fv, logf)
    return y, rows


def setup_inputs(seed: int = 0) -> dict:
    key = jax.random.key(seed)
    ks = iter(jax.random.split(key, 40))
    n_pages = PAST_LEN // PAGE_SIZE
    in_use = DEC_BATCH * n_pages
    n_pool = in_use + max(1, in_use // 4)
    f32 = jnp.float32
    L = DEPTH

    def nrm(shape, scale=1.0):
        return jax.random.normal(next(ks), shape, f32) * scale

    def gain(shape):
        return 1.0 + 0.05 * nrm(shape)

    page_table = jax.random.permutation(next(ks), n_pool)[:in_use].reshape(DEC_BATCH, n_pages).astype(jnp.int32)
    return {
        'x_prompt': nrm((BATCH, SEQ, D_MODEL)),
        'x_sample': nrm((DEC_BATCH, DEC_SEQ, D_MODEL)),
        'cache_mla_latent': nrm((L, n_pool, PAGE_SIZE, KV_LORA)),
        'cache_mla_rope': nrm((L, n_pool, PAGE_SIZE, MLA_ROPE)),
        'cache_fox_k': nrm((L, n_pool, PAGE_SIZE, FOX_HEADS, FOX_HD)),
        'cache_fox_v': nrm((L, n_pool, PAGE_SIZE, FOX_HEADS, FOX_HD)),
        'cache_fox_logf': jax.nn.log_sigmoid(nrm((L, n_pool, PAGE_SIZE, FOX_HEADS)) + FORGET_BIAS_INIT),
        'page_table': page_table,
        'meta_tokens': nrm((NUM_META, D_MODEL)),
        'g_norm1': gain((L, D_MODEL)),
        'w_in': nrm((L, D_MODEL, IN_WIDTH), D_MODEL ** -0.5),
        'g_q_lora': gain((L, Q_LORA)),
        'g_kv_lora': gain((L, KV_LORA)),
        'w_uq': nrm((L, Q_LORA, MLA_HEADS * MLA_QK), Q_LORA ** -0.5),
        'w_ukv': nrm((L, KV_LORA, MLA_HEADS * (MLA_NOPE + MLA_V)), KV_LORA ** -0.5),
        'g_mla_qn': gain((L, MLA_QK)),
        'g_mla_kn': gain((L, MLA_QK)),
        'g_fox_qn': gain((L, FOX_HD)),
        'g_fox_kn': gain((L, FOX_HD)),
        'b_forget': FORGET_BIAS_INIT + 0.1 * nrm((L, FOX_HEADS)),
        'g_mla_out': gain((L, MLA_HEADS * MLA_V)),
        'g_fox_out': gain((L, FOX_HEADS * FOX_HD)),
        'w_out': nrm((L, MIX_WIDTH, D_MODEL), MIX_WIDTH ** -0.5),
        'g_norm2': gain((L, D_MODEL)),
        'w_peer_query': nrm((L, D_MODEL, PEER_HEADS * PEER_DK), D_MODEL ** -0.5),
        'peer_keys_a': nrm((L, PEER_HEADS, PEER_NKEYS, PEER_DK // 2), (PEER_DK // 2) ** -0.5),
        'peer_keys_b': nrm((L, PEER_HEADS, PEER_NKEYS, PEER_DK // 2), (PEER_DK // 2) ** -0.5),
        'peer_u': nrm((L, PEER_EXPERTS, D_MODEL), D_MODEL ** -0.5),
        'peer_v': nrm((L, PEER_EXPERTS, D_MODEL), PEER_HEADS ** -0.5),
    }


def reference(x_prompt, x_sample, cache_mla_latent, cache_mla_rope, cache_fox_k, cache_fox_v,
              cache_fox_logf, page_table, meta_tokens, g_norm1, w_in, g_q_lora, g_kv_lora, w_uq,
              w_ukv, g_mla_qn, g_mla_kn, g_fox_qn, g_fox_kn, b_forget, g_mla_out, g_fox_out, w_out,
              g_norm2, w_peer_query, peer_keys_a, peer_keys_b, peer_u, peer_v):
    B, _, D = x_prompt.shape
    meta = jnp.broadcast_to(meta_tokens[None].astype(x_prompt.dtype), (B, NUM_META, D))
    xp = jnp.concatenate([meta, x_prompt], axis=1)
    xs = x_sample
    p_rows, s_rows = [], []
    for l in range(DEPTH):
        p = {
            'g_norm1': g_norm1[l], 'w_in': w_in[l], 'g_q_lora': g_q_lora[l], 'g_kv_lora': g_kv_lora[l],
            'w_uq': w_uq[l], 'w_ukv': w_ukv[l], 'g_mla_qn': g_mla_qn[l], 'g_mla_kn': g_mla_kn[l],
            'g_fox_qn': g_fox_qn[l], 'g_fox_kn': g_fox_kn[l], 'b_forget': b_forget[l],
            'g_mla_out': g_mla_out[l], 'g_fox_out': g_fox_out[l], 'w_out': w_out[l],
            'g_norm2': g_norm2[l], 'w_peer_query': w_peer_query[l], 'peer_keys_a': peer_keys_a[l],
            'peer_keys_b': peer_keys_b[l], 'peer_u': peer_u[l], 'peer_v': peer_v[l],
        }
        xp, rp = prompt_layer(xp, p)
        xs, rs = sample_layer(xs, cache_mla_latent[l], cache_mla_rope[l], cache_fox_k[l],
                              cache_fox_v[l], cache_fox_logf[l], page_table, p)
        p_rows.append(rp)
        s_rows.append(rs)
    p_lat, p_rope, p_fk, p_fv, p_lf = [jnp.stack(r) for r in zip(*p_rows)]
    s_lat, s_rope, s_fk, s_fv, s_lf = [jnp.stack(r) for r in zip(*s_rows)]
    y_prompt = xp[:, NUM_META:]
    y_sample = xs
    return (y_prompt, y_sample, p_lat, p_rope, p_fk, p_fv, p_lf, s_lat, s_rope, s_fk, s_fv, s_lf)
```

```python
import functools

import numpy as np
import jax
import jax.numpy as jnp
from jax import lax
from jax.experimental import pallas as pl
from jax.experimental.pallas import tpu as pltpu

NUM_META = 16
MLA_HEADS = 8
MLA_NOPE = 64
MLA_ROPE = 32
MLA_QK = MLA_NOPE + MLA_ROPE
MLA_V = 64
Q_LORA = 768
KV_LORA = 256
FOX_HEADS = 8
FOX_HD = 64
PEER_HEADS = 8
PEER_NKEYS = 128
PEER_DK = 256
PEER_TOPK = 16
ROPE_THETA = 10000.0
EPS = 1e-6
NEG_INF = -1e30

LANES = 128
HP = 128
VMEM_LIMIT = 56 * 1024 * 1024

F32 = jnp.float32
BF16 = jnp.bfloat16


def _dot(a, b):
    return jnp.dot(a, b, preferred_element_type=F32)


def _dot_nt(a, b):
    return lax.dot_general(a, b, (((1,), (1,)), ((), ())), preferred_element_type=F32)


def _split2(x):
    hi = x.astype(BF16)
    lo = (x - hi.astype(F32)).astype(BF16)
    return hi, lo


def _split3(x):
    a = x.astype(BF16)
    r = x - a.astype(F32)
    b = r.astype(BF16)
    c = (r - b.astype(F32)).astype(BF16)
    return a, b, c


def _dot_exact_rhs(m, x):
    a, b, c = _split3(x)
    return _dot(m, a) + _dot(m, b) + _dot(m, c)


def _rms(x, g, n):
    ss = jnp.sum(x * x, axis=-1, keepdims=True)
    return x * lax.rsqrt(ss * (1.0 / n) + EPS) * g


def _log_sigmoid(x):
    return jnp.minimum(x, 0.0) - jnp.log1p(jnp.exp(-jnp.abs(x)))


def _rope128(x, cosp, sina, sinb):
    return x * cosp + pltpu.roll(x, 112, 1) * sina + pltpu.roll(x, 16, 1) * sinb


W_CQ, W_CKV, W_KPE, W_FQ, W_FK, W_FV, W_MISC = 768, 256, 1024, 1024, 1024, 512, 128
O_CQ = 0
O_CKV = O_CQ + W_CQ
O_KPE = O_CKV + W_CKV
O_FQ = O_KPE + W_KPE
O_FK = O_FQ + W_FQ
O_FV = O_FK + W_FK
O_MISC = O_FV + W_FV
W_IN_P = O_MISC + W_MISC
MISC_LF = 32


def _feat_kernel(x_ref, g1_ref, win_ref, gq_ref, gkv_ref, wuq_ref, wukv_ref, gmq_ref, gmk_ref,
                 gfq_ref, gfk_ref, b3_ref, cos_ref, sina_ref, sinb_ref, ltri_ref, pq_ref, pk_ref,
                 cq_ref, ck_ref,
                 lat_ref, kpe_ref, fkp_ref, fv_ref, lf_ref, qa_ref, ka_ref, va_ref, carry_ref):
    @pl.when(pl.program_id(0) == 0)
    def _():
        carry_ref[...] = jnp.zeros_like(carry_ref)

    x = x_ref[...]
    hb = _rms(x, g1_ref[...], x.shape[-1]).astype(BF16)
    cosp, sina, sinb = cos_ref[...], sina_ref[...], sinb_ref[...]

    c_q = _dot(hb, win_ref[:, O_CQ:O_CQ + W_CQ])
    c_q = _rms(c_q, gq_ref[...], Q_LORA).astype(BF16)
    q = _dot(c_q, wuq_ref[...])
    scale_m = MLA_QK ** -0.5
    for h in range(MLA_HEADS):
        qh = _rms(q[:, h * HP:(h + 1) * HP], gmq_ref[...], MLA_QK)
        qh = _rope128(qh, cosp, sina, sinb) * scale_m
        qa_ref[:, h * HP:(h + 1) * HP] = qh.astype(BF16)

    c_kv = _dot(hb, win_ref[:, O_CKV:O_CKV + W_CKV])
    c_kv = _rms(c_kv, gkv_ref[...], KV_LORA)
    lat_ref[...] = c_kv
    kv = _dot(c_kv.astype(BF16), wukv_ref[...])
    kpe_t = _dot(hb, win_ref[:, O_KPE:O_KPE + W_KPE])
    for h in range(MLA_HEADS):
        kh = kv[:, h * HP:(h + 1) * HP] + kpe_t[:, h * HP:(h + 1) * HP]
        kh = _rope128(_rms(kh, gmk_ref[...], MLA_QK), cosp, sina, sinb)
        ka_ref[:, h * HP:(h + 1) * HP] = kh.astype(BF16)
    nv = MLA_HEADS * MLA_V
    va_ref[:, 0:nv] = kv[:, MLA_HEADS * HP:MLA_HEADS * HP + nv].astype(BF16)

    misc = _dot(hb, win_ref[:, O_MISC:O_MISC + W_MISC])
    kpe_ref[...] = misc[:, 0:MLA_ROPE]
    logf = _log_sigmoid(misc + b3_ref[...])
    lf_ref[...] = logf[:, MISC_LF:MISC_LF + FOX_HEADS]
    lane = lax.broadcasted_iota(jnp.int32, logf.shape, 1)
    lf3 = jnp.where((lane >= MISC_LF) & (lane < MISC_LF + 3 * FOX_HEADS), logf, 0.0)
    cum = _dot_exact_rhs(ltri_ref[...], lf3) + carry_ref[...]
    carry_ref[...] = cum[cum.shape[0] - 1:, :]
    c1, c2, c3 = _split3(cum)
    csel = jnp.where(lane < MISC_LF + FOX_HEADS, c1, jnp.where(lane < MISC_LF + 2 * FOX_HEADS, c2, c3))
    aug_q = _dot(csel, pq_ref[...]) + cq_ref[...]
    aug_k = _dot(csel, pk_ref[...]) + ck_ref[...]

    fq = _dot(hb, win_ref[:, O_FQ:O_FQ + W_FQ])
    fk = _dot(hb, win_ref[:, O_FK:O_FK + W_FK])
    scale_f = FOX_HD ** -0.5
    off = MLA_HEADS * HP
    for h in range(FOX_HEADS):
        sl = slice(h * HP, (h + 1) * HP)
        qh = _rms(fq[:, sl], gfq_ref[...], FOX_HD) * scale_f
        qa_ref[:, off + h * HP:off + (h + 1) * HP] = (qh + aug_q[:, sl]).astype(BF16)
        kh = _rms(fk[:, sl], gfk_ref[...], FOX_HD)
        fkp_ref[:, sl] = kh
        ka_ref[:, off + h * HP:off + (h + 1) * HP] = (kh + aug_k[:, sl]).astype(BF16)
    fv = _dot(hb, win_ref[:, O_FV:O_FV + W_FV])
    fv_ref[...] = fv
    va_ref[:, nv:nv + W_FV] = fv.astype(BF16)


def _pad_heads(w, n_heads, width):
    k = w.shape[0]
    w = w.reshape(k, n_heads, width)
    return jnp.pad(w, ((0, 0), (0, 0), (0, HP - width))).reshape(k, n_heads * HP)


def _feat_weights(w_in, w_uq, w_ukv, g_mla_qn, g_mla_kn, g_fox_qn, g_fox_kn, b_forget):
    d = w_in.shape[0]
    o = np.cumsum([0, Q_LORA, KV_LORA, MLA_ROPE, FOX_HEADS * FOX_HD, FOX_HEADS * FOX_HD,
                   FOX_HEADS * FOX_HD, FOX_HEADS])
    w_cq, w_ckv, w_kpe, w_fq, w_fk, w_fv, w_fl = [w_in[:, o[i]:o[i + 1]] for i in range(7)]
    kpe_grp = jnp.pad(w_kpe, ((0, 0), (MLA_NOPE, HP - MLA_QK)))
    misc = jnp.concatenate([w_kpe, w_fl, w_fl, w_fl], axis=1)
    misc = jnp.pad(misc, ((0, 0), (0, W_MISC - misc.shape[1])))
    win_p = jnp.concatenate([w_cq, w_ckv, jnp.tile(kpe_grp, (1, MLA_HEADS)),
                             _pad_heads(w_fq, FOX_HEADS, FOX_HD), _pad_heads(w_fk, FOX_HEADS, FOX_HD),
                             w_fv, misc], axis=1).astype(BF16)
    assert win_p.shape == (d, W_IN_P)
    wuq_p = _pad_heads(w_uq, MLA_HEADS, MLA_QK).astype(BF16)
    kvw = w_ukv.reshape(KV_LORA, MLA_HEADS, MLA_NOPE + MLA_V)
    wk = kvw[:, :, :MLA_NOPE].reshape(KV_LORA, MLA_HEADS * MLA_NOPE)
    wv = kvw[:, :, MLA_NOPE:].reshape(KV_LORA, MLA_HEADS * MLA_V)
    wukv_p = jnp.concatenate([_pad_heads(wk, MLA_HEADS, MLA_NOPE), wv], axis=1).astype(BF16)
    pad1 = lambda g, n: jnp.pad(g, (0, HP - n)).reshape(1, HP)
    b3 = jnp.pad(jnp.tile(b_forget, 3), (MISC_LF, W_MISC - MISC_LF - 3 * FOX_HEADS)).reshape(1, W_MISC)
    return (win_p, wuq_p, wukv_p, wk.astype(BF16), wv.astype(BF16), pad1(g_mla_qn, MLA_QK),
            pad1(g_mla_kn, MLA_QK), pad1(g_fox_qn, FOX_HD), pad1(g_fox_kn, FOX_HD), b3)


def _aug_constants():
    pq = np.zeros((W_MISC, FOX_HEADS * HP), np.float32)
    pk = np.zeros((W_MISC, FOX_HEADS * HP), np.float32)
    cq = np.zeros((1, FOX_HEADS * HP), np.float32)
    ck = np.zeros((1, FOX_HEADS * HP), np.float32)
    for h in range(FOX_HEADS):
        for j in range(3):
            pq[MISC_LF + j * FOX_HEADS + h, h * HP + FOX_HD + j] = 1.0
            ck[0, h * HP + FOX_HD + j] = 1.0
            pk[MISC_LF + j * FOX_HEADS + h, h * HP + FOX_HD + 3 + j] = -1.0
            cq[0, h * HP + FOX_HD + 3 + j] = 1.0
    return jnp.asarray(pq, BF16), jnp.asarray(pk, BF16), jnp.asarray(cq), jnp.asarray(ck)


def _rope_tables(pos):
    half = MLA_ROPE // 2
    inv = jnp.power(ROPE_THETA, -jnp.arange(half, dtype=F32) / half)
    ang = pos.astype(F32)[:, None] * inv[None, :]
    return jnp.cos(ang), jnp.sin(ang)


def _feat_call(rows, pos, tm, g_norm1, g_q_lora, g_kv_lora, fw):
    win_p, wuq_p, wukv_p, _, _, gmq, gmk, gfq, gfk, b3 = fw
    r, d = rows.shape
    cos, sin = _rope_tables(pos)
    n = cos.shape[0]
    one, zero = jnp.ones((n, MLA_NOPE), F32), jnp.zeros((n, HP - MLA_QK), F32)
    z16 = jnp.zeros_like(sin)
    cosp = jnp.concatenate([one, cos, cos, one[:, :HP - MLA_QK]], axis=1)
    sina = jnp.concatenate([zero, zero, -sin, z16, zero], axis=1)
    sinb = jnp.concatenate([zero, zero, z16, sin, zero], axis=1)
    ltri = jnp.asarray(np.tril(np.ones((tm, tm), np.float32)), BF16)
    pq, pk, cq, ck = _aug_constants()
    row = lambda w: pl.BlockSpec((tm, w), lambda i: (i, 0))
    full = lambda a: pl.BlockSpec(a.shape, lambda i: (0,) * a.ndim)
    g1 = g_norm1.reshape(1, d)
    gq = g_q_lora.reshape(1, Q_LORA)
    gkv = g_kv_lora.reshape(1, KV_LORA)
    consts = (g1, win_p, gq, gkv, wuq_p, wukv_p, gmq, gmk, gfq, gfk, b3)
    tail = (ltri, pq, pk, cq, ck)
    nh = MLA_HEADS + FOX_HEADS
    out_shape = (
        jax.ShapeDtypeStruct((r, KV_LORA), F32), jax.ShapeDtypeStruct((r, MLA_ROPE), F32),
        jax.ShapeDtypeStruct((r, FOX_HEADS * HP), F32), jax.ShapeDtypeStruct((r, W_FV), F32),
        jax.ShapeDtypeStruct((r, FOX_HEADS), F32), jax.ShapeDtypeStruct((r, nh * HP), BF16),
        jax.ShapeDtypeStruct((r, nh * HP), BF16), jax.ShapeDtypeStruct((r, MLA_HEADS * MLA_V + W_FV), BF16))
    return pl.pallas_call(
        _feat_kernel,
        grid=(r // tm,),
        in_specs=[row(d)] + [full(a) for a in consts] + [row(HP), row(HP), row(HP)] + [full(a) for a in tail],
        out_specs=[row(s.shape[1]) for s in out_shape],
        out_shape=out_shape,
        scratch_shapes=[pltpu.VMEM((1, W_MISC), F32)],
        compiler_params=pltpu.CompilerParams(dimension_semantics=("arbitrary",),
                                             vmem_limit_bytes=VMEM_LIMIT),
        name="feat",
    )(rows, *consts, cosp, sina, sinb, *tail)


def _attn_kernel(q_ref, k_ref, v_ref, o_ref, *, tq, tk):
    qi = pl.program_id(1)
    nsub = tq // tk
    outs = []
    for j in range(2):
        q = q_ref[:, j * HP:(j + 1) * HP]

        def block(kb, carry, masked):
            m, l, acc = carry
            k = k_ref[pl.ds(pl.multiple_of(kb * tk, tk), tk), j * HP:(j + 1) * HP]
            v = v_ref[pl.ds(pl.multiple_of(kb * tk, tk), tk), :]
            s = _dot_nt(q, k)
            if masked:
                rowi = qi * tq + lax.broadcasted_iota(jnp.int32, s.shape, 0)
                coli = kb * tk + lax.broadcasted_iota(jnp.int32, s.shape, 1)
                s = jnp.where(coli <= rowi, s, NEG_INF)
            m_new = jnp.maximum(m, jnp.max(s, axis=-1, keepdims=True))
            alpha = jnp.exp(m - m_new)
            p = jnp.exp(s - m_new)
            l = alpha * l + jnp.sum(p, axis=-1, keepdims=True)
            acc = alpha * acc + _dot(p.astype(BF16), v)
            return m_new, l, acc

        init = (jnp.full((tq, 1), NEG_INF, F32), jnp.zeros((tq, 1), F32), jnp.zeros((tq, 2 * MLA_V), F32))
        carry = lax.fori_loop(0, qi * nsub, lambda kb, c: block(kb, c, False), init)
        for dsub in range(nsub):
            carry = block(qi * nsub + dsub, carry, True)
        m, l, acc = carry
        outs.append(acc / l)
    lane = lax.broadcasted_iota(jnp.int32, outs[0].shape, 1)
    o_ref[...] = jnp.where(lane < MLA_V, outs[0], outs[1])


def _attn_call(qa, ka, va, tp, tq, tk):
    npairs = qa.shape[1] // (2 * HP)
    return pl.pallas_call(
        functools.partial(_attn_kernel, tq=tq, tk=tk),
        grid=(npairs, tp // tq),
        in_specs=[pl.BlockSpec((tq, 2 * HP), lambda p, i: (i, p)),
                  pl.BlockSpec((tp, 2 * HP), lambda p, i: (0, p)),
                  pl.BlockSpec((tp, 2 * MLA_V), lambda p, i: (0, p))],
        out_specs=pl.BlockSpec((tq, 2 * MLA_V), lambda p, i: (i, p)),
        out_shape=jax.ShapeDtypeStruct((tp, npairs * 2 * MLA_V), F32),
        compiler_params=pltpu.CompilerParams(dimension_semantics=("arbitrary", "arbitrary"),
                                             vmem_limit_bytes=VMEM_LIMIT),
        name="attn",
    )(qa, ka, va)


def _softmax_step(s, v, m_ref, l_ref, acc_ref, v_transposed=False):
    m_old = m_ref[...]
    m_new = jnp.maximum(m_old, jnp.max(s, axis=-1, keepdims=True))
    alpha = jnp.exp(m_old - m_new)
    p = jnp.exp(s - m_new)
    l_ref[...] = alpha * l_ref[...] + jnp.sum(p, axis=-1, keepdims=True)
    pv = _dot_nt(p.astype(BF16), v) if v_transposed else _dot(p.astype(BF16), v)
    acc_ref[...] = alpha * acc_ref[...] + pv
    m_ref[...] = m_new


def _head_diag(x, hm):
    ns = x.shape[0] // MLA_HEADS
    return jnp.sum(x.reshape(ns, MLA_HEADS, x.shape[1]) * hm[None], axis=1)


def _decode_kernel(pt_ref, *refs, pg):
    del pt_ref
    page_refs = refs[:5 * pg]
    (cos_ref, sin_ref, qbm_ref, qbf_ref, qn_ref, qr_ref, qf_ref, kan_ref, van_ref, lfn_ref,
     wk_ref, wv_ref, gn_ref, gr_ref, bexp_ref, lstrict_ref,
     ltri8_ref, es_ref, fh_ref, hm_ref, causal_ref,
     o_ref, mm_ref, lm_ref, am_ref, mf_ref, lfs_ref, af_ref, carry_ref, cnq_ref) = refs[5 * pg:]
    g = pl.program_id(1)
    nm = MLA_HEADS * HP

    @pl.when(g == 0)
    def _():
        causal = causal_ref[...]
        ka = kan_ref[0]
        va = van_ref[0]
        s_m = jnp.where(causal > 0, _dot_nt(qbm_ref[0], ka[:, :nm]), NEG_INF)
        mm_ref[...] = jnp.full_like(mm_ref, NEG_INF)
        lm_ref[...] = jnp.zeros_like(lm_ref)
        am_ref[...] = jnp.zeros_like(am_ref)
        _softmax_step(s_m, va[:, :MLA_HEADS * MLA_V], mm_ref, lm_ref, am_ref)
        cn = _dot_exact_rhs(ltri8_ref[...], lfn_ref[0])
        cnq = jnp.sum(_dot_exact_rhs(es_ref[...], cn) * fh_ref[...].astype(F32), axis=-1, keepdims=True)
        a, b, c = _split3(cn)
        fh = fh_ref[...]
        cnk = _dot_nt(fh, a) + _dot_nt(fh, b) + _dot_nt(fh, c)
        s_f = _dot_nt(qbf_ref[0], ka[:, nm:]) + cnq - cnk
        s_f = jnp.where(causal > 0, s_f, NEG_INF)
        mf_ref[...] = jnp.full_like(mf_ref, NEG_INF)
        lfs_ref[...] = jnp.zeros_like(lfs_ref)
        af_ref[...] = jnp.zeros_like(af_ref)
        _softmax_step(s_f, va[:, MLA_HEADS * MLA_V:], mf_ref, lfs_ref, af_ref)
        cnq_ref[...] = cnq
        carry_ref[...] = jnp.zeros_like(carry_ref)

    qn, qr, qf = qn_ref[0], qr_ref[0], qf_ref[0]
    bexp = bexp_ref[...]
    half = MLA_ROPE // 2
    for i in reversed(range(pg)):
        lat_ref, rope_ref, fk_ref, fv_ref, lf_ref = page_refs[5 * i:5 * i + 5]
        lat = lat_ref[0].astype(BF16)
        kn = _dot(lat, wk_ref[...])
        v = _dot(lat, wv_ref[...]).astype(BF16)
        kpe = rope_ref[0]
        k2h, k2l = _split2(kn * kn)
        ssq = _dot_nt(bexp, k2h) + _dot_nt(bexp, k2l) + jnp.sum(kpe * kpe, axis=0, keepdims=True)
        r_t = lax.rsqrt(ssq * (1.0 / MLA_QK) + EPS)
        kg = kpe * gr_ref[...]
        sw = jnp.concatenate([kg[half:], kg[:half]], axis=0)
        cols = slice(i * LANES, (i + 1) * LANES)
        kr = kg * cos_ref[:, cols] + sw * sin_ref[:, cols]
        s_m = (_dot_nt(qn, (kn * gn_ref[...]).astype(BF16)) + _dot(qr, kr.astype(BF16))) * r_t
        _softmax_step(s_m, v, mm_ref, lm_ref, am_ref)
        lf = lf_ref[0]
        a, b, c = _split3(lf)
        lstrict = lstrict_ref[...]
        s_t = _dot(a, lstrict) + _dot(b, lstrict) + _dot(c, lstrict) + carry_ref[...]
        carry_ref[...] = carry_ref[...] + jnp.sum(lf, axis=-1, keepdims=True)
        dec = jnp.concatenate([s_t] * (qf.shape[0] // FOX_HEADS), axis=0) + cnq_ref[...]
        s_f = _dot(qf, fk_ref[0].astype(BF16)) + dec
        _softmax_step(s_f, fv_ref[0].astype(BF16), mf_ref, lfs_ref, af_ref, v_transposed=True)

    @pl.when(g == pl.num_programs(1) - 1)
    def _():
        hm = hm_ref[...]
        nv = MLA_HEADS * MLA_V
        o_ref[0, :, 0:nv] = _head_diag(am_ref[...] / lm_ref[...], hm)
        o_ref[0, :, nv:2 * nv] = _head_diag(af_ref[...] / lfs_ref[...], hm)


def _decode_constants(ns):
    nr = ns * MLA_HEADS
    hh = np.arange(nr) % MLA_HEADS
    ss = np.arange(nr) // MLA_HEADS
    bexp = (np.arange(MLA_HEADS * MLA_NOPE)[None, :] // MLA_NOPE == hh[:, None]).astype(np.float32)
    lstrict = np.tril(np.ones((LANES, LANES), np.float32), -1)
    ltri8 = np.tril(np.ones((ns, ns), np.float32))
    es = (np.arange(ns)[None, :] == ss[:, None]).astype(np.float32)
    fh = (np.arange(FOX_HEADS)[None, :] == hh[:, None]).astype(np.float32)
    hm = (np.arange(MLA_HEADS * MLA_V)[None, :] // MLA_V == np.arange(MLA_HEADS)[:, None]).astype(np.float32)
    causal = (np.arange(ns)[None, :] <= ss[:, None]).astype(np.float32)
    bf = lambda a: jnp.asarray(a, BF16)
    return (bf(bexp), bf(lstrict), bf(ltri8), bf(es), bf(fh), jnp.asarray(hm), jnp.asarray(causal))


def _block_diag_rows(q, width):
    b, s, h, w = q.shape
    eye = jnp.eye(h, dtype=q.dtype)
    return (q[:, :, :, None, :] * eye[None, None, :, :, None]).reshape(b, s * h, h * w)


def _decode_call(qa_s, ka_s, va_s, lf_s, pools, page_table, fw, g_mla_kn, pg):
    lat_pool, rope_pool, fk_pool, fv_pool, lf_pool = pools
    nb, npages = page_table.shape
    ns = qa_s.shape[0] // nb
    nr = ns * MLA_HEADS
    nch = npages // pg
    page = lat_pool.shape[1]
    assert page == LANES and npages % pg == 0
    rope_pool = jnp.swapaxes(rope_pool, 1, 2)
    fk_pool = jnp.transpose(fk_pool, (0, 2, 3, 1)).reshape(fk_pool.shape[0], -1, page)
    fv_pool = jnp.transpose(fv_pool, (0, 2, 3, 1)).reshape(fv_pool.shape[0], -1, page)
    lf_pool = jnp.swapaxes(lf_pool, 1, 2)
    past = npages * page
    nm = MLA_HEADS * HP
    _, _, _, wk, wv, _, _, _, _, _ = fw
    q4 = qa_s.reshape(nb, ns, MLA_HEADS + FOX_HEADS, HP)
    qm, qfx = q4[:, :, :MLA_HEADS], q4[:, :, MLA_HEADS:, :FOX_HD]
    qbm = _block_diag_rows(qm, HP)
    qbf = _block_diag_rows(jnp.pad(qfx, ((0, 0), (0, 0), (0, 0), (0, HP - FOX_HD))), HP)
    qn = _block_diag_rows(qm[..., :MLA_NOPE], MLA_NOPE)
    qr = qm[..., MLA_NOPE:MLA_QK].reshape(nb, nr, MLA_ROPE)
    qf = _block_diag_rows(qfx, FOX_HD)
    kan = ka_s.reshape(nb, ns, ka_s.shape[1])
    van = va_s.reshape(nb, ns, va_s.shape[1])
    lfn = lf_s.reshape(nb, ns, FOX_HEADS)
    cos, sin = _rope_tables(jnp.arange(past, dtype=jnp.int32))
    cos2 = jnp.concatenate([cos, cos], axis=1).T
    sin2 = jnp.concatenate([-sin, sin], axis=1).T
    gn = jnp.tile(g_mla_kn[:MLA_NOPE], MLA_HEADS).reshape(1, MLA_HEADS * MLA_NOPE)
    gr = jnp.broadcast_to(g_mla_kn[MLA_NOPE:MLA_QK].reshape(MLA_ROPE, 1), (MLA_ROPE, LANES))
    consts = (wk, wv, gn, gr) + _decode_constants(ns)
    pt_flat = page_table.reshape(-1).astype(jnp.int32)

    def page_spec(pool, i):
        return pl.BlockSpec((1,) + pool.shape[1:],
                            lambda b, g, pt: (pt[b * npages + (nch - 1 - g) * pg + i], 0, 0))

    page_specs, page_args = [], []
    for i in range(pg):
        for pool in (lat_pool, rope_pool, fk_pool, fv_pool, lf_pool):
            page_specs.append(page_spec(pool, i))
            page_args.append(pool)
    tab = pl.BlockSpec((MLA_ROPE, pg * page), lambda b, g, pt: (0, nch - 1 - g))
    seq = lambda a: pl.BlockSpec((1,) + a.shape[1:], lambda b, g, pt: (b,) + (0,) * (a.ndim - 1))
    full = lambda a: pl.BlockSpec(a.shape, lambda b, g, pt: (0,) * a.ndim)
    per_seq = (qbm, qbf, qn, qr, qf, kan, van, lfn)
    nv = MLA_HEADS * MLA_V
    grid_spec = pltpu.PrefetchScalarGridSpec(
        num_scalar_prefetch=1,
        grid=(nb, nch),
        in_specs=page_specs + [tab, tab] + [seq(a) for a in per_seq] + [full(a) for a in consts],
        out_specs=pl.BlockSpec((1, ns, 2 * nv), lambda b, g, pt: (b, 0, 0)),
        scratch_shapes=[pltpu.VMEM((nr, 1), F32), pltpu.VMEM((nr, 1), F32), pltpu.VMEM((nr, nv), F32),
                        pltpu.VMEM((nr, 1), F32), pltpu.VMEM((nr, 1), F32), pltpu.VMEM((nr, nv), F32),
                        pltpu.VMEM((FOX_HEADS, 1), F32), pltpu.VMEM((nr, 1), F32)])
    out = pl.pallas_call(
        functools.partial(_decode_kernel, pg=pg),
        grid_spec=grid_spec,
        out_shape=jax.ShapeDtypeStruct((nb, ns, 2 * nv), F32),
        compiler_params=pltpu.CompilerParams(dimension_semantics=("arbitrary", "arbitrary"),
                                             vmem_limit_bytes=VMEM_LIMIT),
        name="decode",
    )(pt_flat, *page_args, cos2, sin2, *per_seq, *consts)
    return out.reshape(nb * ns, 2 * nv)


def _merge_kernel(x_ref, o_ref, gmo_ref, gfo_ref, wout_ref, g2_ref, wpq_ref, ka_ref, kb_ref,
                  x1_ref, h2_ref, sa_ref, sb_ref):
    o = o_ref[...]
    nv = MLA_HEADS * MLA_V
    mixed = jnp.concatenate([_rms(o[:, :nv], gmo_ref[...], nv).astype(BF16),
                             _rms(o[:, nv:], gfo_ref[...], o.shape[1] - nv).astype(BF16)], axis=1)
    x1 = x_ref[...] + _dot(mixed, wout_ref[...])
    x1_ref[...] = x1
    h2 = _rms(x1, g2_ref[...], x1.shape[1]).astype(BF16)
    h2_ref[...] = h2
    pq = _dot(h2, wpq_ref[...])
    half = PEER_DK // 2
    for h in range(PEER_HEADS):
        qa = pq[:, h * PEER_DK:h * PEER_DK + half].astype(BF16)
        qb = pq[:, h * PEER_DK + half:(h + 1) * PEER_DK].astype(BF16)
        sa_ref[h * PEER_NKEYS:(h + 1) * PEER_NKEYS, :] = _dot_nt(ka_ref[h], qa)
        sb_ref[h * PEER_NKEYS:(h + 1) * PEER_NKEYS, :] = _dot_nt(kb_ref[h], qb)


def _merge_call(x, o, tm, g_mla_out, g_fox_out, w_out, g_norm2, w_peer_query, keys_a, keys_b):
    r, d = x.shape
    row = lambda w: pl.BlockSpec((tm, w), lambda i: (i, 0))
    full = lambda a: pl.BlockSpec(a.shape, lambda i: (0,) * a.ndim)
    col = pl.BlockSpec((PEER_HEADS * PEER_NKEYS, tm), lambda i: (0, i))
    consts = (g_mla_out.reshape(1, -1), g_fox_out.reshape(1, -1), w_out.astype(BF16),
              g_norm2.reshape(1, -1), w_peer_query.astype(BF16), keys_a.astype(BF16), keys_b.astype(BF16))
    nk = PEER_HEADS * PEER_NKEYS
    return pl.pallas_call(
        _merge_kernel,
        grid=(r // tm,),
        in_specs=[row(d), row(o.shape[1])] + [full(a) for a in consts],
        out_specs=[row(d), row(d), col, col],
        out_shape=(jax.ShapeDtypeStruct((r, d), F32), jax.ShapeDtypeStruct((r, d), BF16),
                   jax.ShapeDtypeStruct((nk, r), F32), jax.ShapeDtypeStruct((nk, r), F32)),
        compiler_params=pltpu.CompilerParams(dimension_semantics=("arbitrary",),
                                             vmem_limit_bytes=VMEM_LIMIT),
        name="merge",
    )(x, o, *consts)


_CAND = [(i, j) for i in range(PEER_TOPK) for j in range(PEER_TOPK) if (i + 1) * (j + 1) <= PEER_TOPK]
_NCAND = -(-len(_CAND) // 8) * 8


def _extract_max(cur, order):
    m = jnp.max(cur, axis=0, keepdims=True)
    first = jnp.min(jnp.where(cur == m, order, 1e9), axis=0, keepdims=True)
    return m, order == first


def _topk_kernel(sa_ref, sb_ref, ci_ref, cj_ref, rb_ref, ca_ref, ea_ref, eb_ref):
    k = PEER_TOPK
    tn = sa_ref.shape[1]
    key_order = lax.broadcasted_iota(jnp.int32, (PEER_NKEYS, tn), 0).astype(F32)
    cand_order = lax.broadcasted_iota(jnp.int32, (_NCAND, tn), 0).astype(F32)
    ci = ci_ref[...]
    cj = cj_ref[...]
    sa, sb = sa_ref[...], sb_ref[...]
    ranks, vals = [], []
    for s in (sa, sb):
        cur = s
        rank = jnp.full(s.shape, float(k), F32)
        v = []
        for i in range(k):
            m, hot = _extract_max(cur, key_order)
            v.append(m)
            rank = jnp.where(hot, float(i), rank)
            cur = jnp.where(hot, -jnp.inf, cur)
        ranks.append(rank)
        vals.append(v)
    rank_a, rank_b = ranks
    va, vb = vals
    cand = jnp.full((_NCAND, tn), -jnp.inf, F32)
    for i in range(k):
        cand = jnp.where(ci == float(i), va[i], cand)
    cur = jnp.full((_NCAND, tn), -jnp.inf, F32)
    for j in range(k):
        cur = jnp.where(cj == float(j), cand + vb[j], cur)
    top = va[0] + vb[0]
    z = jnp.zeros((1, tn), F32)
    sel = jnp.zeros((_NCAND, tn), F32)
    for _ in range(k):
        m, hot = _extract_max(cur, cand_order)
        z = z + jnp.exp(m - top)
        sel = jnp.where(hot, 1.0, sel)
        cur = jnp.where(hot, -jnp.inf, cur)
    ca = jnp.zeros(sa.shape, F32)
    for i in range(k):
        c_i = jnp.sum(jnp.where(ci == float(i), sel, 0.0), axis=0, keepdims=True)
        ca = jnp.where(rank_a == float(i), c_i, ca)
    rb_ref[...] = rank_b
    ca_ref[...] = ca
    ea_ref[...] = jnp.exp(sa - va[0]) / z
    eb_ref[...] = jnp.exp(sb - vb[0])


def _topk_call(sa, sb, tn):
    nk, r = sa.shape
    ci = np.full((_NCAND, tn), -1.0, np.float32)
    cj = np.full((_NCAND, tn), -1.0, np.float32)
    for n, (i, j) in enumerate(_CAND):
        ci[n, :], cj[n, :] = i, j
    col = pl.BlockSpec((PEER_NKEYS, tn), lambda i, h: (h, i))
    full = pl.BlockSpec((_NCAND, tn), lambda i, h: (0, 0))
    return pl.pallas_call(
        _topk_kernel,
        grid=(r // tn, nk // PEER_NKEYS),
        in_specs=[col, col, full, full],
        out_specs=[col] * 4,
        out_shape=tuple(jax.ShapeDtypeStruct((nk, r), F32) for _ in range(4)),
        compiler_params=pltpu.CompilerParams(dimension_semantics=("arbitrary", "arbitrary"),
                                             vmem_limit_bytes=VMEM_LIMIT),
        name="topk",
    )(sa, sb, jnp.asarray(ci), jnp.asarray(cj))


def _peer_kernel(h2_ref, x1_ref, u_ref, vt_ref, rb_ref, ca_ref, ea_ref, eb_ref, y_ref, acc_ref, w_ref,
                 *, a_per_step):
    c = pl.program_id(1)

    @pl.when(c == 0)
    def _():
        acc_ref[...] = jnp.zeros_like(acc_ref)

    s = _dot_nt(u_ref[...], h2_ref[...])
    act = 0.5 * s * (1.0 + lax.erf(s * np.float32(np.sqrt(0.5))))
    for aa in range(a_per_step):
        gate = jnp.zeros((PEER_NKEYS, s.shape[1]), F32)
        for h in range(PEER_HEADS):
            row = pl.ds(h * PEER_NKEYS + c * a_per_step + aa, 1)
            sl = slice(h * PEER_NKEYS, (h + 1) * PEER_NKEYS)
            gate = gate + jnp.where(rb_ref[sl, :] < ca_ref[row, :], eb_ref[sl, :], 0.0) * ea_ref[row, :]
        blk = slice(aa * PEER_NKEYS, (aa + 1) * PEER_NKEYS)
        w_ref[blk, :] = (gate * act[blk, :]).astype(BF16)
    acc_ref[...] += _dot(vt_ref[...], w_ref[...])

    @pl.when(c == pl.num_programs(1) - 1)
    def _():
        y_ref[...] = x1_ref[...] + acc_ref[...].T


def _peer_call(h2, x1, u, vt, tables, tm, a_per_step):
    r, d = h2.shape
    ne = u.shape[0]
    ec = a_per_step * PEER_NKEYS
    nk = PEER_HEADS * PEER_NKEYS
    row = pl.BlockSpec((tm, d), lambda i, c: (i, 0))
    col = pl.BlockSpec((nk, tm), lambda i, c: (0, i))
    return pl.pallas_call(
        functools.partial(_peer_kernel, a_per_step=a_per_step),
        grid=(r // tm, ne // ec),
        in_specs=[row, row, pl.BlockSpec((ec, d), lambda i, c: (c, 0)),
                  pl.BlockSpec((d, ec), lambda i, c: (0, c)), col, col, col, col],
        out_specs=row,
        out_shape=jax.ShapeDtypeStruct((r, d), F32),
        scratch_shapes=[pltpu.VMEM((d, tm), F32), pltpu.VMEM((ec, tm), BF16)],
        compiler_params=pltpu.CompilerParams(dimension_semantics=("arbitrary", "arbitrary"),
                                             vmem_limit_bytes=VMEM_LIMIT),
        name="peer",
    )(h2, x1, u, vt, *tables)


def _round_up(n, m):
    return -(-n // m) * m


def _layer(xp, xs, pools, page_table, p, tiles):
    t, d = xp.shape
    nb, ns, _ = xs.shape
    past = page_table.shape[1] * pools[0].shape[1]
    tp = _round_up(t, tiles["tq"])
    n_s = nb * ns
    r = tp + n_s
    assert r % tiles["tm"] == 0 and r % tiles["tp"] == 0
    rows = jnp.concatenate([xp, jnp.zeros((tp - t, d), xp.dtype), xs.reshape(n_s, d)], axis=0)
    pos = jnp.concatenate([jnp.arange(tp, dtype=jnp.int32),
                           past + (jnp.arange(n_s, dtype=jnp.int32) % ns)])
    fw = _feat_weights(p['w_in'], p['w_uq'], p['w_ukv'], p['g_mla_qn'], p['g_mla_kn'],
                       p['g_fox_qn'], p['g_fox_kn'], p['b_forget'])
    lat, kpe, fkp, fv, lf, qa, ka, va = _feat_call(rows, pos, tiles["tm"], p['g_norm1'], p['g_q_lora'],
                                                   p['g_kv_lora'], fw)
    o_p = _attn_call(qa, ka, va, tp, tiles["tq"], tiles["tk"])
    o_s = _decode_call(qa[tp:], ka[tp:], va[tp:], lf[tp:], pools, page_table, fw, p['g_mla_kn'],
                       tiles["pg"])
    o = jnp.concatenate([o_p, o_s], axis=0)
    x1, h2, sa, sb = _merge_call(rows, o, tiles["tm"], p['g_mla_out'], p['g_fox_out'], p['w_out'],
                                 p['g_norm2'], p['w_peer_query'], p['peer_keys_a'], p['peer_keys_b'])
    tables = _topk_call(sa, sb, tiles["tn"])
    y = _peer_call(h2, x1, p['peer_u'].astype(BF16), p['peer_v'].astype(BF16).T, tables,
                   tiles["tp"], tiles["ap"])
    fk = fkp.reshape(r, FOX_HEADS, HP)[:, :, :FOX_HD]
    fvh = fv.reshape(r, FOX_HEADS, FOX_HD)
    prm = lambda a: a[:t][None]
    smp = lambda a: a[tp:].reshape((nb, ns) + a.shape[1:])
    outs = (lat, kpe, fk, fvh, lf)
    return (y[:t], y[tp:].reshape(nb, ns, d), tuple(prm(a) for a in outs), tuple(smp(a) for a in outs))


_TILES = dict(tm=256, tq=512, tk=512, pg=8, tn=256, tp=512, ap=8)


def kernel(x_prompt, x_sample, cache_mla_latent, cache_mla_rope, cache_fox_k, cache_fox_v, cache_fox_logf,
           page_table, meta_tokens, g_norm1, w_in, g_q_lora, g_kv_lora, w_uq, w_ukv, g_mla_qn, g_mla_kn,
           g_fox_qn, g_fox_kn, b_forget, g_mla_out, g_fox_out, w_out, g_norm2, w_peer_query, peer_keys_a,
           peer_keys_b, peer_u, peer_v):
    tiles = _TILES
    depth = w_in.shape[0]
    assert depth == 1 and x_prompt.shape[0] == 1
    xp = jnp.concatenate([meta_tokens.astype(x_prompt.dtype), x_prompt[0]], axis=0)
    xs = x_sample
    p_rows, s_rows = [], []
    for l in range(depth):
        p = {
            'g_norm1': g_norm1[l], 'w_in': w_in[l], 'g_q_lora': g_q_lora[l], 'g_kv_lora': g_kv_lora[l],
            'w_uq': w_uq[l], 'w_ukv': w_ukv[l], 'g_mla_qn': g_mla_qn[l], 'g_mla_kn': g_mla_kn[l],
            'g_fox_qn': g_fox_qn[l], 'g_fox_kn': g_fox_kn[l], 'b_forget': b_forget[l],
            'g_mla_out': g_mla_out[l], 'g_fox_out': g_fox_out[l], 'w_out': w_out[l],
            'g_norm2': g_norm2[l], 'w_peer_query': w_peer_query[l], 'peer_keys_a': peer_keys_a[l],
            'peer_keys_b': peer_keys_b[l], 'peer_u': peer_u[l], 'peer_v': peer_v[l],
        }
        pools = (cache_mla_latent[l], cache_mla_rope[l], cache_fox_k[l], cache_fox_v[l], cache_fox_logf[l])
        yp, ys, rp, rs = _layer(xp, xs, pools, page_table, p, tiles)
        xp, xs = yp, ys
        p_rows.append(rp)
        s_rows.append(rs)
    p_lat, p_rope, p_fk, p_fv, p_lf = [jnp.stack(r) for r in zip(*p_rows)]
    s_lat, s_rope, s_fk, s_fv, s_lf = [jnp.stack(r) for r in zip(*s_rows)]
    return (xp[NUM_META:][None], xs, p_lat, p_rope, p_fk, p_fv, p_lf, s_lat, s_rope, s_fk, s_fv, s_lf)
```

```python
import functools

import numpy as np
import jax
import jax.numpy as jnp
from jax import lax
from jax.experimental import pallas as pl
from jax.experimental.pallas import tpu as pltpu

NUM_META = 16
MLA_HEADS = 8
MLA_NOPE = 64
MLA_ROPE = 32
MLA_QK = MLA_NOPE + MLA_ROPE
MLA_V = 64
Q_LORA = 768
KV_LORA = 256
FOX_HEADS = 8
FOX_HD = 64
PEER_HEADS = 8
PEER_NKEYS = 128
PEER_DK = 256
PEER_TOPK = 16
ROPE_THETA = 10000.0
EPS = 1e-6
NEG_INF = -1e30
LOG2E = float(np.log2(np.e))

LANES = 128
HP = 128
VMEM_LIMIT = 56 * 1024 * 1024

F32 = jnp.float32
BF16 = jnp.bfloat16


def _dot(a, b):
    return jnp.dot(a, b, preferred_element_type=F32)


def _dot_nt(a, b):
    return lax.dot_general(a, b, (((1,), (1,)), ((), ())), preferred_element_type=F32)


def _split2(x):
    hi = x.astype(BF16)
    lo = (x - hi.astype(F32)).astype(BF16)
    return hi, lo


def _split3(x):
    a = x.astype(BF16)
    r = x - a.astype(F32)
    b = r.astype(BF16)
    c = (r - b.astype(F32)).astype(BF16)
    return a, b, c


def _dot_exact_rhs(m, x):
    a, b, c = _split3(x)
    return _dot(m, a) + _dot(m, b) + _dot(m, c)


def _rms(x, g, n):
    ss = jnp.sum(x * x, axis=-1, keepdims=True)
    return x * lax.rsqrt(ss * (1.0 / n) + EPS) * g


def _log_sigmoid(x):
    return jnp.minimum(x, 0.0) - jnp.log1p(jnp.exp(-jnp.abs(x)))


def _rope128(x, cosp, sina, sinb):
    return x * cosp + pltpu.roll(x, 112, 1) * sina + pltpu.roll(x, 16, 1) * sinb


W_CQ, W_CKV, W_KPE, W_FQ, W_FK, W_FV, W_MISC = 768, 256, 1024, 1024, 1024, 1024, 128
O_CQ = 0
O_CKV = O_CQ + W_CQ
O_KPE = O_CKV + W_CKV
O_FQ = O_KPE + W_KPE
O_FK = O_FQ + W_FQ
O_FV = O_FK + W_FK
O_MISC = O_FV + W_FV
W_IN_P = O_MISC + W_MISC
MISC_LF = 32


def _feat_kernel(x_ref, g1_ref, win_ref, gq_ref, gkv_ref, wuq_ref, wukv_ref, gmq_ref, gmk_ref,
                 gfq_ref, gfk_ref, b3_ref, cos_ref, sina_ref, sinb_ref, ltri_ref, pq_ref, pk_ref,
                 cq_ref, ck_ref, one_ref,
                 lat_ref, kpe_ref, fkp_ref, fvp_ref, lf_ref, qa_ref, ka_ref, va_ref, carry_ref):
    @pl.when(pl.program_id(0) == 0)
    def _():
        carry_ref[...] = jnp.zeros_like(carry_ref)

    x = x_ref[...]
    hb = _rms(x, g1_ref[...], x.shape[-1]).astype(BF16)
    cosp, sina, sinb = cos_ref[...], sina_ref[...], sinb_ref[...]

    c_q = _dot(hb, win_ref[:, O_CQ:O_CQ + W_CQ])
    c_q = _rms(c_q, gq_ref[...], Q_LORA).astype(BF16)
    q = _dot(c_q, wuq_ref[...])
    scale_m = MLA_QK ** -0.5 * LOG2E
    for h in range(MLA_HEADS):
        qh = _rms(q[:, h * HP:(h + 1) * HP], gmq_ref[...], MLA_QK)
        qh = _rope128(qh, cosp, sina, sinb) * scale_m
        qa_ref[:, h * HP:(h + 1) * HP] = qh.astype(BF16)

    c_kv = _dot(hb, win_ref[:, O_CKV:O_CKV + W_CKV])
    c_kv = _rms(c_kv, gkv_ref[...], KV_LORA)
    lat_ref[...] = c_kv
    kv = _dot(c_kv.astype(BF16), wukv_ref[...])
    kpe_t = _dot(hb, win_ref[:, O_KPE:O_KPE + W_KPE])
    off = MLA_HEADS * HP
    for h in range(MLA_HEADS):
        kh = kv[:, h * HP:(h + 1) * HP] + kpe_t[:, h * HP:(h + 1) * HP]
        kh = _rope128(_rms(kh, gmk_ref[...], MLA_QK), cosp, sina, sinb)
        ka_ref[:, h * HP:(h + 1) * HP] = kh.astype(BF16)
        va_ref[:, h * HP:(h + 1) * HP] = (kv[:, off + h * HP:off + (h + 1) * HP] + one_ref[...]).astype(BF16)

    misc = _dot(hb, win_ref[:, O_MISC:O_MISC + W_MISC])
    kpe_ref[...] = misc[:, 0:MLA_ROPE]
    logf = _log_sigmoid(misc + b3_ref[...])
    lf_ref[...] = logf[:, MISC_LF:MISC_LF + FOX_HEADS]
    lane = lax.broadcasted_iota(jnp.int32, logf.shape, 1)
    lf3 = jnp.where((lane >= MISC_LF) & (lane < MISC_LF + 3 * FOX_HEADS), logf, 0.0)
    cum = _dot_exact_rhs(ltri_ref[...], lf3) + carry_ref[...]
    carry_ref[...] = cum[cum.shape[0] - 1:, :]
    c1, c2, c3 = _split3(cum * LOG2E)
    csel = jnp.where(lane < MISC_LF + FOX_HEADS, c1, jnp.where(lane < MISC_LF + 2 * FOX_HEADS, c2, c3))
    aug_q = _dot(csel, pq_ref[...]) + cq_ref[...]
    aug_k = _dot(csel, pk_ref[...]) + ck_ref[...]

    fq = _dot(hb, win_ref[:, O_FQ:O_FQ + W_FQ])
    fk = _dot(hb, win_ref[:, O_FK:O_FK + W_FK])
    scale_f = FOX_HD ** -0.5 * LOG2E
    for h in range(FOX_HEADS):
        sl = slice(h * HP, (h + 1) * HP)
        qh = _rms(fq[:, sl], gfq_ref[...], FOX_HD) * scale_f
        qa_ref[:, off + h * HP:off + (h + 1) * HP] = (qh + aug_q[:, sl]).astype(BF16)
        kh = _rms(fk[:, sl], gfk_ref[...], FOX_HD)
        fkp_ref[:, sl] = kh
        ka_ref[:, off + h * HP:off + (h + 1) * HP] = (kh + aug_k[:, sl]).astype(BF16)
    fvp = _dot(hb, win_ref[:, O_FV:O_FV + W_FV])
    fvp_ref[...] = fvp
    for h in range(FOX_HEADS):
        sl = slice(h * HP, (h + 1) * HP)
        va_ref[:, off + h * HP:off + (h + 1) * HP] = (fvp[:, sl] + one_ref[...]).astype(BF16)


def _pad_heads(w, n_heads, width):
    k = w.shape[0]
    w = w.reshape(k, n_heads, width)
    return jnp.pad(w, ((0, 0), (0, 0), (0, HP - width))).reshape(k, n_heads * HP)


def _feat_weights(w_in, w_uq, w_ukv, g_mla_qn, g_mla_kn, g_fox_qn, g_fox_kn, b_forget):
    d = w_in.shape[0]
    o = np.cumsum([0, Q_LORA, KV_LORA, MLA_ROPE, FOX_HEADS * FOX_HD, FOX_HEADS * FOX_HD,
                   FOX_HEADS * FOX_HD, FOX_HEADS])
    w_cq, w_ckv, w_kpe, w_fq, w_fk, w_fv, w_fl = [w_in[:, o[i]:o[i + 1]] for i in range(7)]
    kpe_grp = jnp.pad(w_kpe, ((0, 0), (MLA_NOPE, HP - MLA_QK)))
    misc = jnp.concatenate([w_kpe, w_fl, w_fl, w_fl], axis=1)
    misc = jnp.pad(misc, ((0, 0), (0, W_MISC - misc.shape[1])))
    win_p = jnp.concatenate([w_cq, w_ckv, jnp.tile(kpe_grp, (1, MLA_HEADS)),
                             _pad_heads(w_fq, FOX_HEADS, FOX_HD), _pad_heads(w_fk, FOX_HEADS, FOX_HD),
                             _pad_heads(w_fv, FOX_HEADS, FOX_HD), misc], axis=1).astype(BF16)
    assert win_p.shape == (d, W_IN_P)
    wuq_p = _pad_heads(w_uq, MLA_HEADS, MLA_QK).astype(BF16)
    kvw = w_ukv.reshape(KV_LORA, MLA_HEADS, MLA_NOPE + MLA_V)
    wk = kvw[:, :, :MLA_NOPE].reshape(KV_LORA, MLA_HEADS * MLA_NOPE)
    wv = kvw[:, :, MLA_NOPE:].reshape(KV_LORA, MLA_HEADS * MLA_V)
    wukv_p = jnp.concatenate([_pad_heads(wk, MLA_HEADS, MLA_NOPE), _pad_heads(wv, MLA_HEADS, MLA_V)],
                             axis=1).astype(BF16)
    pad1 = lambda g, n: jnp.pad(g, (0, HP - n)).reshape(1, HP)
    b3 = jnp.pad(jnp.tile(b_forget, 3), (MISC_LF, W_MISC - MISC_LF - 3 * FOX_HEADS)).reshape(1, W_MISC)
    return (win_p, wuq_p, wukv_p, wk.astype(BF16), wv.astype(BF16), pad1(g_mla_qn, MLA_QK),
            pad1(g_mla_kn, MLA_QK), pad1(g_fox_qn, FOX_HD), pad1(g_fox_kn, FOX_HD), b3)


def _aug_constants():
    pq = np.zeros((W_MISC, FOX_HEADS * HP), np.float32)
    pk = np.zeros((W_MISC, FOX_HEADS * HP), np.float32)
    cq = np.zeros((1, FOX_HEADS * HP), np.float32)
    ck = np.zeros((1, FOX_HEADS * HP), np.float32)
    for h in range(FOX_HEADS):
        for j in range(3):
            pq[MISC_LF + j * FOX_HEADS + h, h * HP + FOX_HD + j] = 1.0
            ck[0, h * HP + FOX_HD + j] = 1.0
            pk[MISC_LF + j * FOX_HEADS + h, h * HP + FOX_HD + 3 + j] = -1.0
            cq[0, h * HP + FOX_HD + 3 + j] = 1.0
    return jnp.asarray(pq, BF16), jnp.asarray(pk, BF16), jnp.asarray(cq), jnp.asarray(ck)


def _rope_tables(pos):
    half = MLA_ROPE // 2
    inv = jnp.power(ROPE_THETA, -jnp.arange(half, dtype=F32) / half)
    ang = pos.astype(F32)[:, None] * inv[None, :]
    return jnp.cos(ang), jnp.sin(ang)


def _feat_call(rows, pos, tm, g_norm1, g_q_lora, g_kv_lora, fw):
    win_p, wuq_p, wukv_p, _, _, gmq, gmk, gfq, gfk, b3 = fw
    r, d = rows.shape
    cos, sin = _rope_tables(pos)
    n = cos.shape[0]
    one, zero = jnp.ones((n, MLA_NOPE), F32), jnp.zeros((n, HP - MLA_QK), F32)
    z16 = jnp.zeros_like(sin)
    cosp = jnp.concatenate([one, cos, cos, one[:, :HP - MLA_QK]], axis=1)
    sina = jnp.concatenate([zero, zero, -sin, z16, zero], axis=1)
    sinb = jnp.concatenate([zero, zero, z16, sin, zero], axis=1)
    ltri = jnp.asarray(np.tril(np.ones((tm, tm), np.float32)), BF16)
    pq, pk, cq, ck = _aug_constants()
    one = jnp.zeros((1, HP), F32).at[0, MLA_V].set(1.0)
    row = lambda w: pl.BlockSpec((tm, w), lambda i: (i, 0))
    full = lambda a: pl.BlockSpec(a.shape, lambda i: (0,) * a.ndim)
    g1 = g_norm1.reshape(1, d)
    gq = g_q_lora.reshape(1, Q_LORA)
    gkv = g_kv_lora.reshape(1, KV_LORA)
    consts = (g1, win_p, gq, gkv, wuq_p, wukv_p, gmq, gmk, gfq, gfk, b3)
    tail = (ltri, pq, pk, cq, ck, one)
    nh = MLA_HEADS + FOX_HEADS
    out_shape = (
        jax.ShapeDtypeStruct((r, KV_LORA), F32), jax.ShapeDtypeStruct((r, MLA_ROPE), F32),
        jax.ShapeDtypeStruct((r, FOX_HEADS * HP), F32), jax.ShapeDtypeStruct((r, FOX_HEADS * HP), F32),
        jax.ShapeDtypeStruct((r, FOX_HEADS), F32), jax.ShapeDtypeStruct((r, nh * HP), BF16),
        jax.ShapeDtypeStruct((r, nh * HP), BF16), jax.ShapeDtypeStruct((r, nh * HP), BF16))
    return pl.pallas_call(
        _feat_kernel,
        grid=(r // tm,),
        in_specs=[row(d)] + [full(a) for a in consts] + [row(HP), row(HP), row(HP)] + [full(a) for a in tail],
        out_specs=[row(s.shape[1]) for s in out_shape],
        out_shape=out_shape,
        scratch_shapes=[pltpu.VMEM((1, W_MISC), F32)],
        compiler_params=pltpu.CompilerParams(dimension_semantics=("arbitrary",),
                                             vmem_limit_bytes=VMEM_LIMIT),
        name="feat",
    )(rows, *consts, cosp, sina, sinb, *tail)


def _attn_kernel(q_ref, k_ref, v_ref, o_ref, m_ref, acc_ref, *, tq, tk):
    assert tq == tk
    qi = pl.program_id(1)
    qs = [q_ref[:, j * HP:(j + 1) * HP] for j in range(2)]

    def block(kb, carry, masked):
        start = pl.multiple_of(kb * tk, tk)
        new = []
        for j in range(2):
            m, acc = carry[2 * j], carry[2 * j + 1]
            k = k_ref[pl.ds(start, tk), j * HP:(j + 1) * HP]
            v = v_ref[pl.ds(start, tk), j * HP:(j + 1) * HP]
            s = _dot_nt(qs[j], k)
            if masked:
                rowi = qi * tq + lax.broadcasted_iota(jnp.int32, s.shape, 0)
                coli = kb * tk + lax.broadcasted_iota(jnp.int32, s.shape, 1)
                s = jnp.where(coli <= rowi, s, NEG_INF)
            m_new = jnp.maximum(m, jnp.max(s, axis=-1, keepdims=True))
            alpha = jnp.exp2(m - m_new)
            p = jnp.exp2(s - m_new)
            new += [m_new, alpha * acc + _dot(p.astype(BF16), v)]
        return tuple(new)

    def save(carry):
        for j in range(2):
            m_ref[j] = carry[2 * j]
            acc_ref[j] = carry[2 * j + 1]

    def load():
        return (m_ref[0], acc_ref[0], m_ref[1], acc_ref[1])

    init = (jnp.full((tq, 1), NEG_INF, F32), jnp.zeros((tq, HP), F32)) * 2
    save(lax.fori_loop(0, qi // 2, lambda i, c: block(2 * i + 1, block(2 * i, c, False), False), init))

    @pl.when(qi % 2 == 1)
    def _():
        save(block(qi - 1, load(), False))

    carry = block(qi, load(), True)
    outs = [carry[2 * j + 1] / carry[2 * j + 1][:, MLA_V:MLA_V + 1] for j in range(2)]
    lane = lax.broadcasted_iota(jnp.int32, outs[0].shape, 1)
    o_ref[...] = jnp.where(lane < MLA_V, outs[0], pltpu.roll(outs[1], MLA_V, 1))


def _attn_call(qa, ka, va, tp, tq, tk):
    npairs = qa.shape[1] // (2 * HP)
    return pl.pallas_call(
        functools.partial(_attn_kernel, tq=tq, tk=tk),
        grid=(npairs, tp // tq),
        in_specs=[pl.BlockSpec((tq, 2 * HP), lambda p, i: (i, p)),
                  pl.BlockSpec((tp, 2 * HP), lambda p, i: (0, p)),
                  pl.BlockSpec((tp, 2 * HP), lambda p, i: (0, p))],
        out_specs=pl.BlockSpec((tq, 2 * MLA_V), lambda p, i: (i, p)),
        out_shape=jax.ShapeDtypeStruct((tp, npairs * 2 * MLA_V), F32),
        scratch_shapes=[pltpu.VMEM((2, tq, 1), F32), pltpu.VMEM((2, tq, HP), F32)],
        compiler_params=pltpu.CompilerParams(dimension_semantics=("arbitrary", "arbitrary"),
                                             vmem_limit_bytes=VMEM_LIMIT),
        name="attn",
    )(qa, ka, va)


def _softmax_step(s, pv, m_ref, l_ref, acc_ref):
    m_old = m_ref[...]
    m_new = jnp.maximum(m_old, jnp.max(s, axis=-1, keepdims=True))
    alpha = jnp.exp2(m_old - m_new)
    p = jnp.exp2(s - m_new)
    l_ref[...] = alpha * l_ref[...] + jnp.sum(p, axis=-1, keepdims=True)
    acc_ref[...] = alpha * acc_ref[...] + pv(p.astype(BF16))
    m_ref[...] = m_new


def _head_diag(x, hm):
    ns = x.shape[0] // MLA_HEADS
    return jnp.sum(x.reshape(ns, MLA_HEADS, x.shape[1]) * hm[None], axis=1)


def _decode_kernel(pt_ref, *refs, pg):
    del pt_ref
    lat_refs, rope_refs, fk_refs, fv_refs, lf_refs = [refs[j * pg:(j + 1) * pg] for j in range(5)]
    (cos_ref, sin_ref, qbm_ref, qbf_ref, qn_ref, qr_ref, qf_ref, kan_ref, latn_ref, fvn_ref, lfn_ref,
     wk_ref, wv_ref, gn_ref, gr_ref, bexp_ref, lstrict_ref,
     ltri8_ref, es_ref, fh_ref, hm_ref, causal_ref,
     o_ref, mm_ref, lm_ref, am_ref, mf_ref, lfs_ref, af_ref, carry_ref, cnq_ref, qng_ref) = refs[5 * pg:]
    g = pl.program_id(1)
    nm = MLA_HEADS * HP

    @pl.when(g == 0)
    def _():
        causal = causal_ref[...]
        ka = kan_ref[0]
        s_m = jnp.where(causal > 0, _dot_nt(qbm_ref[0], ka[:, :nm]), NEG_INF)
        mm_ref[...] = jnp.full_like(mm_ref, NEG_INF)
        lm_ref[...] = jnp.zeros_like(lm_ref)
        am_ref[...] = jnp.zeros_like(am_ref)
        lat_n = latn_ref[0].astype(BF16)
        _softmax_step(s_m, lambda p: _dot(p, lat_n), mm_ref, lm_ref, am_ref)
        cn = _dot_exact_rhs(ltri8_ref[...], lfn_ref[0])
        cnq = jnp.sum(_dot_exact_rhs(es_ref[...], cn) * fh_ref[...].astype(F32), axis=-1, keepdims=True)
        a, b, c = _split3(cn)
        fh = fh_ref[...]
        cnk = _dot_nt(fh, a) + _dot_nt(fh, b) + _dot_nt(fh, c)
        s_f = _dot_nt(qbf_ref[0], ka[:, nm:]) + (cnq - cnk) * LOG2E
        s_f = jnp.where(causal > 0, s_f, NEG_INF)
        mf_ref[...] = jnp.full_like(mf_ref, NEG_INF)
        lfs_ref[...] = jnp.zeros_like(lfs_ref)
        af_ref[...] = jnp.zeros_like(af_ref)
        fv_n = fvn_ref[0].astype(BF16)
        _softmax_step(s_f, lambda p: _dot(p, fv_n), mf_ref, lfs_ref, af_ref)
        cnq_ref[...] = cnq * LOG2E
        carry_ref[...] = jnp.zeros_like(carry_ref)
        qng_ref[...] = (qn_ref[0].astype(F32) * gn_ref[...]).astype(BF16)

    half = MLA_ROPE // 2
    nrep = qf_ref.shape[1] // FOX_HEADS
    lat = jnp.concatenate([r[0].astype(BF16) for r in lat_refs], axis=0)
    kn = _dot(lat, wk_ref[...])
    kpe = jnp.concatenate([r[0] for r in rope_refs], axis=1)
    ssq = _dot_nt(bexp_ref[...], (kn * kn).astype(BF16)) + jnp.sum(kpe * kpe, axis=0, keepdims=True)
    r_t = lax.rsqrt(ssq * (1.0 / MLA_QK) + EPS)
    kg = kpe * gr_ref[...]
    sw = jnp.concatenate([kg[half:], kg[:half]], axis=0)
    kr = kg * cos_ref[...] + sw * sin_ref[...]
    s_m = (_dot_nt(qng_ref[...], kn.astype(BF16)) + _dot(qr_ref[0], kr.astype(BF16))) * r_t
    _softmax_step(s_m, lambda p: _dot(p, lat), mm_ref, lm_ref, am_ref)
    lf = jnp.concatenate([r[0] for r in lf_refs], axis=0)
    a, b, c = _split3(lf)
    lstrict = lstrict_ref[...]
    within = _dot(a, lstrict) + _dot(b, lstrict) + _dot(c, lstrict)
    tot = jnp.sum(lf, axis=-1, keepdims=True)
    run = carry_ref[...]
    decs = [None] * pg
    for i in reversed(range(pg)):
        rows = slice(i * FOX_HEADS, (i + 1) * FOX_HEADS)
        decs[i] = jnp.concatenate([within[rows] + run] * nrep, axis=0)
        run = run + tot[rows]
    carry_ref[...] = run
    dec = jnp.concatenate(decs, axis=1) * LOG2E + cnq_ref[...]
    fk = jnp.concatenate([r[0].astype(BF16) for r in fk_refs], axis=1)
    s_f = _dot(qf_ref[0], fk) + dec
    fv = jnp.concatenate([r[0].astype(BF16) for r in fv_refs], axis=1)
    _softmax_step(s_f, lambda p: _dot_nt(p, fv), mf_ref, lfs_ref, af_ref)

    @pl.when(g == pl.num_programs(1) - 1)
    def _():
        hm = hm_ref[...]
        nv = MLA_HEADS * MLA_V
        o_lat = (am_ref[...] / lm_ref[...]).astype(BF16)
        o_ref[0, :, 0:nv] = _head_diag(_dot(o_lat, wv_ref[...]), hm)
        o_ref[0, :, nv:2 * nv] = _head_diag(af_ref[...] / lfs_ref[...], hm)


def _decode_constants(ns):
    nr = ns * MLA_HEADS
    hh = np.arange(nr) % MLA_HEADS
    ss = np.arange(nr) // MLA_HEADS
    bexp = (np.arange(MLA_HEADS * MLA_NOPE)[None, :] // MLA_NOPE == hh[:, None]).astype(np.float32)
    lstrict = np.tril(np.ones((LANES, LANES), np.float32), -1)
    ltri8 = np.tril(np.ones((ns, ns), np.float32))
    es = (np.arange(ns)[None, :] == ss[:, None]).astype(np.float32)
    fh = (np.arange(FOX_HEADS)[None, :] == hh[:, None]).astype(np.float32)
    hm = (np.arange(MLA_HEADS * MLA_V)[None, :] // MLA_V == np.arange(MLA_HEADS)[:, None]).astype(np.float32)
    causal = (np.arange(ns)[None, :] <= ss[:, None]).astype(np.float32)
    bf = lambda a: jnp.asarray(a, BF16)
    return (bf(bexp), bf(lstrict), bf(ltri8), bf(es), bf(fh), jnp.asarray(hm), jnp.asarray(causal))


def _block_diag_rows(q, width):
    b, s, h, w = q.shape
    eye = jnp.eye(h, dtype=q.dtype)
    return (q[:, :, :, None, :] * eye[None, None, :, :, None]).reshape(b, s * h, h * w)


def _decode_call(qa_s, ka_s, lat_s, fv_s, lf_s, pools, page_table, fw, g_mla_kn, pg):
    lat_pool, rope_pool, fk_pool, fv_pool, lf_pool = pools
    nb, npages = page_table.shape
    ns = qa_s.shape[0] // nb
    nr = ns * MLA_HEADS
    nch = npages // pg
    page = lat_pool.shape[1]
    assert page == LANES and npages % pg == 0
    rope_pool = jnp.swapaxes(rope_pool, 1, 2)
    fk_pool = jnp.transpose(fk_pool, (0, 2, 3, 1)).reshape(fk_pool.shape[0], -1, page)
    fv_pool = jnp.transpose(fv_pool, (0, 2, 3, 1)).reshape(fv_pool.shape[0], -1, page)
    lf_pool = jnp.swapaxes(lf_pool, 1, 2)
    past = npages * page
    nm = MLA_HEADS * HP
    _, _, _, wk, wv, _, _, _, _, _ = fw
    q4 = qa_s.reshape(nb, ns, MLA_HEADS + FOX_HEADS, HP)
    qm, qfx = q4[:, :, :MLA_HEADS], q4[:, :, MLA_HEADS:, :FOX_HD]
    qbm = _block_diag_rows(qm, HP)
    qbf = _block_diag_rows(jnp.pad(qfx, ((0, 0), (0, 0), (0, 0), (0, HP - FOX_HD))), HP)
    qn = _block_diag_rows(qm[..., :MLA_NOPE], MLA_NOPE)
    qr = qm[..., MLA_NOPE:MLA_QK].reshape(nb, nr, MLA_ROPE)
    qf = _block_diag_rows(qfx, FOX_HD)
    kan = ka_s.reshape(nb, ns, ka_s.shape[1])
    latn = lat_s.reshape(nb, ns, lat_s.shape[1])
    fvn = fv_s.reshape(nb, ns, -1)
    lfn = lf_s.reshape(nb, ns, FOX_HEADS)
    cos, sin = _rope_tables(jnp.arange(past, dtype=jnp.int32))
    cos2 = jnp.concatenate([cos, cos], axis=1).T
    sin2 = jnp.concatenate([-sin, sin], axis=1).T
    gn = jnp.tile(g_mla_kn[:MLA_NOPE], MLA_HEADS).reshape(1, MLA_HEADS * MLA_NOPE)
    gr = jnp.broadcast_to(g_mla_kn[MLA_NOPE:MLA_QK].reshape(MLA_ROPE, 1), (MLA_ROPE, pg * page))
    consts = (wk, wv, gn, gr) + _decode_constants(ns)
    pt_flat = page_table.reshape(-1).astype(jnp.int32)

    def page_spec(pool, i):
        return pl.BlockSpec((1,) + pool.shape[1:],
                            lambda b, g, pt: (pt[b * npages + (nch - 1 - g) * pg + i], 0, 0))

    page_specs, page_args = [], []
    for pool in (lat_pool, rope_pool, fk_pool, fv_pool, lf_pool):
        for i in range(pg):
            page_specs.append(page_spec(pool, i))
            page_args.append(pool)
    tab = pl.BlockSpec((MLA_ROPE, pg * page), lambda b, g, pt: (0, nch - 1 - g))
    seq = lambda a: pl.BlockSpec((1,) + a.shape[1:], lambda b, g, pt: (b,) + (0,) * (a.ndim - 1))
    full = lambda a: pl.BlockSpec(a.shape, lambda b, g, pt: (0,) * a.ndim)
    per_seq = (qbm, qbf, qn, qr, qf, kan, latn, fvn, lfn)
    nv = MLA_HEADS * MLA_V
    grid_spec = pltpu.PrefetchScalarGridSpec(
        num_scalar_prefetch=1,
        grid=(nb, nch),
        in_specs=page_specs + [tab, tab] + [seq(a) for a in per_seq] + [full(a) for a in consts],
        out_specs=pl.BlockSpec((1, ns, 2 * nv), lambda b, g, pt: (b, 0, 0)),
        scratch_shapes=[pltpu.VMEM((nr, 1), F32), pltpu.VMEM((nr, 1), F32), pltpu.VMEM((nr, KV_LORA), F32),
                        pltpu.VMEM((nr, 1), F32), pltpu.VMEM((nr, 1), F32), pltpu.VMEM((nr, nv), F32),
                        pltpu.VMEM((FOX_HEADS, 1), F32), pltpu.VMEM((nr, 1), F32),
                        pltpu.VMEM((nr, MLA_HEADS * MLA_NOPE), BF16)])
    out = pl.pallas_call(
        functools.partial(_decode_kernel, pg=pg),
        grid_spec=grid_spec,
        out_shape=jax.ShapeDtypeStruct((nb, ns, 2 * nv), F32),
        compiler_params=pltpu.CompilerParams(dimension_semantics=("arbitrary", "arbitrary"),
                                             vmem_limit_bytes=VMEM_LIMIT),
        name="decode",
    )(pt_flat, *page_args, cos2, sin2, *per_seq, *consts)
    return out.reshape(nb * ns, 2 * nv)


def _merge_kernel(x_ref, o_ref, gmo_ref, gfo_ref, wout_ref, g2_ref, wpq_ref, ka_ref, kb_ref,
                  x1_ref, h2_ref, sa_ref, sb_ref):
    o = o_ref[...]
    nv = MLA_HEADS * MLA_V
    mixed = jnp.concatenate([_rms(o[:, :nv], gmo_ref[...], nv).astype(BF16),
                             _rms(o[:, nv:], gfo_ref[...], o.shape[1] - nv).astype(BF16)], axis=1)
    x1 = x_ref[...] + _dot(mixed, wout_ref[...])
    x1_ref[...] = x1
    h2 = _rms(x1, g2_ref[...], x1.shape[1]).astype(BF16)
    h2_ref[...] = h2
    pq = _dot(h2, wpq_ref[...])
    half = PEER_DK // 2
    for h in range(PEER_HEADS):
        qa = pq[:, h * PEER_DK:h * PEER_DK + half].astype(BF16)
        qb = pq[:, h * PEER_DK + half:(h + 1) * PEER_DK].astype(BF16)
        sa_ref[h * PEER_NKEYS:(h + 1) * PEER_NKEYS, :] = _dot_nt(ka_ref[h], qa)
        sb_ref[h * PEER_NKEYS:(h + 1) * PEER_NKEYS, :] = _dot_nt(kb_ref[h], qb)


def _merge_call(x, o, tm, g_mla_out, g_fox_out, w_out, g_norm2, w_peer_query, keys_a, keys_b):
    r, d = x.shape
    row = lambda w: pl.BlockSpec((tm, w), lambda i: (i, 0))
    full = lambda a: pl.BlockSpec(a.shape, lambda i: (0,) * a.ndim)
    col = pl.BlockSpec((PEER_HEADS * PEER_NKEYS, tm), lambda i: (0, i))
    consts = (g_mla_out.reshape(1, -1), g_fox_out.reshape(1, -1), w_out.astype(BF16),
              g_norm2.reshape(1, -1), w_peer_query.astype(BF16), keys_a.astype(BF16), keys_b.astype(BF16))
    nk = PEER_HEADS * PEER_NKEYS
    return pl.pallas_call(
        _merge_kernel,
        grid=(r // tm,),
        in_specs=[row(d), row(o.shape[1])] + [full(a) for a in consts],
        out_specs=[row(d), row(d), col, col],
        out_shape=(jax.ShapeDtypeStruct((r, d), F32), jax.ShapeDtypeStruct((r, d), BF16),
                   jax.ShapeDtypeStruct((nk, r), F32), jax.ShapeDtypeStruct((nk, r), F32)),
        compiler_params=pltpu.CompilerParams(dimension_semantics=("arbitrary",),
                                             vmem_limit_bytes=VMEM_LIMIT),
        name="merge",
    )(x, o, *consts)


_CAND = [(i, j) for i in range(PEER_TOPK) for j in range(PEER_TOPK) if (i + 1) * (j + 1) <= PEER_TOPK]
_NCAND = -(-len(_CAND) // 8) * 8


def _extract_max(cur, order, first_only):
    m = jnp.max(cur, axis=0, keepdims=True)
    if not first_only:
        return m, cur == m
    first = jnp.min(jnp.where(cur == m, order, 1e9), axis=0, keepdims=True)
    return m, order == first


def _topk_tables(sa, sb, ci, cj, first_only):
    k = PEER_TOPK
    tn = sa.shape[1]
    key_order = lax.broadcasted_iota(jnp.int32, (PEER_NKEYS, tn), 0).astype(F32)
    cand_order = lax.broadcasted_iota(jnp.int32, (_NCAND, tn), 0).astype(F32)
    ranks, vals = [], []
    for s in (sa, sb):
        cur = s
        rank = jnp.full(s.shape, float(k), F32)
        v = []
        for i in range(k):
            m, hot = _extract_max(cur, key_order, first_only)
            v.append(m)
            rank = jnp.where(hot, float(i), rank)
            cur = jnp.where(hot, -jnp.inf, cur)
        ranks.append(rank)
        vals.append(v)
    rank_a, rank_b = ranks
    va, vb = vals
    cand = jnp.full((_NCAND, tn), -jnp.inf, F32)
    for i in range(k):
        cand = jnp.where(ci == float(i), va[i], cand)
    cur = jnp.full((_NCAND, tn), -jnp.inf, F32)
    for j in range(k):
        cur = jnp.where(cj == float(j), cand + vb[j], cur)
    top = va[0] + vb[0]
    z = jnp.zeros((1, tn), F32)
    sel = jnp.zeros((_NCAND, tn), F32)
    for _ in range(k):
        m, hot = _extract_max(cur, cand_order, first_only)
        z = z + jnp.exp(m - top)
        sel = jnp.where(hot, 1.0, sel)
        cur = jnp.where(hot, -jnp.inf, cur)
    ca = jnp.zeros(sa.shape, F32)
    for i in range(k):
        c_i = jnp.sum(jnp.where(ci == float(i), sel, 0.0), axis=0, keepdims=True)
        ca = jnp.where(rank_a == float(i), c_i, ca)
    count = lambda hit: jnp.sum(jnp.where(hit, 1.0, 0.0), axis=0, keepdims=True)
    taken = jnp.maximum(jnp.maximum(count(rank_a < k), count(rank_b < k)), count(sel > 0))
    return rank_b, ca, jnp.exp(sa - va[0]) / z, jnp.exp(sb - vb[0]), taken


def _topk_kernel(sa_ref, sb_ref, ci_ref, cj_ref, rb_ref, ca_ref, ea_ref, eb_ref):
    def run(first_only):
        rank_b, ca, ea, eb, taken = _topk_tables(sa_ref[...], sb_ref[...], ci_ref[...], cj_ref[...],
                                                 first_only)
        rb_ref[...] = rank_b.astype(rb_ref.dtype)
        ca_ref[...] = ca
        ea_ref[...] = ea
        eb_ref[...] = eb.astype(eb_ref.dtype)
        return taken

    taken = run(False)

    @pl.when(jnp.max(taken) > PEER_TOPK)
    def _():
        run(True)


def _topk_call(sa, sb, tn):
    nk, r = sa.shape
    ci = np.full((_NCAND, tn), -1.0, np.float32)
    cj = np.full((_NCAND, tn), -1.0, np.float32)
    for n, (i, j) in enumerate(_CAND):
        ci[n, :], cj[n, :] = i, j
    col = pl.BlockSpec((PEER_NKEYS, tn), lambda i, h: (h, i))
    full = pl.BlockSpec((_NCAND, tn), lambda i, h: (0, 0))
    return pl.pallas_call(
        _topk_kernel,
        grid=(r // tn, nk // PEER_NKEYS),
        in_specs=[col, col, full, full],
        out_specs=[col] * 4,
        out_shape=tuple(jax.ShapeDtypeStruct((nk, r), dt) for dt in (BF16, F32, F32, BF16)),
        compiler_params=pltpu.CompilerParams(dimension_semantics=("arbitrary", "arbitrary"),
                                             vmem_limit_bytes=VMEM_LIMIT),
        name="topk",
    )(sa, sb, jnp.asarray(ci), jnp.asarray(cj))


def _peer_kernel(h2_ref, x1_ref, u_ref, vt_ref, rb_ref, ca_ref, ea_ref, eb_ref, y_ref, acc_ref, w_ref,
                 *, a_per_step):
    c = pl.program_id(1)

    @pl.when(c == 0)
    def _():
        acc_ref[...] = jnp.zeros_like(acc_ref)

    s = _dot_nt(u_ref[...], h2_ref[...])
    act = 0.5 * s * (1.0 + lax.erf(s * np.float32(np.sqrt(0.5))))
    zero = jnp.zeros((), BF16)
    for aa in range(a_per_step):
        gate = jnp.zeros((PEER_NKEYS, s.shape[1]), BF16)
        for h in range(PEER_HEADS):
            row = pl.ds(h * PEER_NKEYS + c * a_per_step + aa, 1)
            sl = slice(h * PEER_NKEYS, (h + 1) * PEER_NKEYS)
            c_a = ca_ref[row, :].astype(BF16)
            e_a = ea_ref[row, :].astype(BF16)
            gate = gate + jnp.where(rb_ref[sl, :] < c_a, eb_ref[sl, :], zero) * e_a
        blk = slice(aa * PEER_NKEYS, (aa + 1) * PEER_NKEYS)
        w_ref[blk, :] = gate * act[blk, :].astype(BF16)
    acc_ref[...] += _dot(vt_ref[...], w_ref[...])

    @pl.when(c == pl.num_programs(1) - 1)
    def _():
        y_ref[...] = x1_ref[...] + acc_ref[...].T


def _peer_call(h2, x1, u, vt, tables, tm, a_per_step):
    r, d = h2.shape
    ne = u.shape[0]
    ec = a_per_step * PEER_NKEYS
    nk = PEER_HEADS * PEER_NKEYS
    row = pl.BlockSpec((tm, d), lambda i, c: (i, 0))
    col = pl.BlockSpec((nk, tm), lambda i, c: (0, i))
    return pl.pallas_call(
        functools.partial(_peer_kernel, a_per_step=a_per_step),
        grid=(r // tm, ne // ec),
        in_specs=[row, row, pl.BlockSpec((ec, d), lambda i, c: (c, 0)),
                  pl.BlockSpec((d, ec), lambda i, c: (0, c)), col, col, col, col],
        out_specs=row,
        out_shape=jax.ShapeDtypeStruct((r, d), F32),
        scratch_shapes=[pltpu.VMEM((d, tm), F32), pltpu.VMEM((ec, tm), BF16)],
        compiler_params=pltpu.CompilerParams(dimension_semantics=("arbitrary", "arbitrary"),
                                             vmem_limit_bytes=VMEM_LIMIT),
        name="peer",
    )(h2, x1, u, vt, *tables)


def _round_up(n, m):
    return -(-n // m) * m


def _layer(xp, xs, pools, page_table, p, tiles):
    t, d = xp.shape
    nb, ns, _ = xs.shape
    past = page_table.shape[1] * pools[0].shape[1]
    tp = _round_up(t, tiles["tq"])
    n_s = nb * ns
    r = tp + n_s
    assert r % tiles["tm"] == 0 and r % tiles["tp"] == 0
    rows = jnp.concatenate([xp, jnp.zeros((tp - t, d), xp.dtype), xs.reshape(n_s, d)], axis=0)
    pos = jnp.concatenate([jnp.arange(tp, dtype=jnp.int32),
                           past + (jnp.arange(n_s, dtype=jnp.int32) % ns)])
    fw = _feat_weights(p['w_in'], p['w_uq'], p['w_ukv'], p['g_mla_qn'], p['g_mla_kn'],
                       p['g_fox_qn'], p['g_fox_kn'], p['b_forget'])
    lat, kpe, fkp, fvp, lf, qa, ka, va = _feat_call(rows, pos, tiles["tm"], p['g_norm1'], p['g_q_lora'],
                                                    p['g_kv_lora'], fw)
    fk = fkp.reshape(r, FOX_HEADS, HP)[:, :, :FOX_HD]
    fvh = fvp.reshape(r, FOX_HEADS, HP)[:, :, :FOX_HD]
    o_p = _attn_call(qa, ka, va, tp, tiles["tq"], tiles["tk"])
    o_s = _decode_call(qa[tp:], ka[tp:], lat[tp:], fvh[tp:], lf[tp:], pools, page_table, fw, p['g_mla_kn'],
                       tiles["pg"])
    o = jnp.concatenate([o_p, o_s], axis=0)
    x1, h2, sa, sb = _merge_call(rows, o, tiles["tm"], p['g_mla_out'], p['g_fox_out'], p['w_out'],
                                 p['g_norm2'], p['w_peer_query'], p['peer_keys_a'], p['peer_keys_b'])
    tables = _topk_call(sa, sb, tiles["tn"])
    y = _peer_call(h2, x1, p['peer_u'].astype(BF16), p['peer_v'].astype(BF16).T, tables,
                   tiles["tp"], tiles["ap"])
    prm = lambda a: a[:t][None]
    smp = lambda a: a[tp:].reshape((nb, ns) + a.shape[1:])
    outs = (lat, kpe, fk, fvh, lf)
    return (y[:t], y[tp:].reshape(nb, ns, d), tuple(prm(a) for a in outs), tuple(smp(a) for a in outs))


_TILES = dict(tm=256, tq=512, tk=512, pg=16, tn=256, tp=512, ap=8)


def kernel(x_prompt, x_sample, cache_mla_latent, cache_mla_rope, cache_fox_k, cache_fox_v, cache_fox_logf,
           page_table, meta_tokens, g_norm1, w_in, g_q_lora, g_kv_lora, w_uq, w_ukv, g_mla_qn, g_mla_kn,
           g_fox_qn, g_fox_kn, b_forget, g_mla_out, g_fox_out, w_out, g_norm2, w_peer_query, peer_keys_a,
           peer_keys_b, peer_u, peer_v):
    tiles = _TILES
    depth = w_in.shape[0]
    assert depth == 1 and x_prompt.shape[0] == 1
    xp = jnp.concatenate([meta_tokens.astype(x_prompt.dtype), x_prompt[0]], axis=0)
    xs = x_sample
    p_rows, s_rows = [], []
    for l in range(depth):
        p = {
            'g_norm1': g_norm1[l], 'w_in': w_in[l], 'g_q_lora': g_q_lora[l], 'g_kv_lora': g_kv_lora[l],
            'w_uq': w_uq[l], 'w_ukv': w_ukv[l], 'g_mla_qn': g_mla_qn[l], 'g_mla_kn': g_mla_kn[l],
            'g_fox_qn': g_fox_qn[l], 'g_fox_kn': g_fox_kn[l], 'b_forget': b_forget[l],
            'g_mla_out': g_mla_out[l], 'g_fox_out': g_fox_out[l], 'w_out': w_out[l],
            'g_norm2': g_norm2[l], 'w_peer_query': w_peer_query[l], 'peer_keys_a': peer_keys_a[l],
            'peer_keys_b': peer_keys_b[l], 'peer_u': peer_u[l], 'peer_v': peer_v[l],
        }
        pools = (cache_mla_latent[l], cache_mla_rope[l], cache_fox_k[l], cache_fox_v[l], cache_fox_logf[l])
        yp, ys, rp, rs = _layer(xp, xs, pools, page_table, p, tiles)
        xp, xs = yp, ys
        p_rows.append(rp)
        s_rows.append(rs)
    p_lat, p_rope, p_fk, p_fv, p_lf = [jnp.stack(r) for r in zip(*p_rows)]
    s_lat, s_rope, s_fk, s_fv, s_lf = [jnp.stack(r) for r in zip(*s_rows)]
    return (xp[NUM_META:][None], xs, p_lat, p_rope, p_fk, p_fv, p_lf, s_lat, s_rope, s_fk, s_fv, s_lf)
```

```python
import functools

import numpy as np
import jax
import jax.numpy as jnp
from jax import lax
from jax.experimental import pallas as pl
from jax.experimental.pallas import tpu as pltpu

NUM_META = 16
MLA_HEADS = 8
MLA_NOPE = 64
MLA_ROPE = 32
MLA_QK = MLA_NOPE + MLA_ROPE
MLA_V = 64
Q_LORA = 768
KV_LORA = 256
FOX_HEADS = 8
FOX_HD = 64
PEER_HEADS = 8
PEER_NKEYS = 128
PEER_DK = 256
PEER_TOPK = 16
ROPE_THETA = 10000.0
EPS = 1e-6
NEG_INF = -1e30
LOG2E = float(np.log2(np.e))

LANES = 128
HP = 128
VMEM_LIMIT = 56 * 1024 * 1024

F32 = jnp.float32
BF16 = jnp.bfloat16


def _dot(a, b):
    return jnp.dot(a, b, preferred_element_type=F32)


def _dot_nt(a, b):
    return lax.dot_general(a, b, (((1,), (1,)), ((), ())), preferred_element_type=F32)


def _split2(x):
    hi = x.astype(BF16)
    lo = (x - hi.astype(F32)).astype(BF16)
    return hi, lo


def _split3(x):
    a = x.astype(BF16)
    r = x - a.astype(F32)
    b = r.astype(BF16)
    c = (r - b.astype(F32)).astype(BF16)
    return a, b, c


def _dot_exact_rhs(m, x):
    a, b, c = _split3(x)
    return _dot(m, a) + _dot(m, b) + _dot(m, c)


def _rms(x, g, n):
    ss = jnp.sum(x * x, axis=-1, keepdims=True)
    return x * lax.rsqrt(ss * (1.0 / n) + EPS) * g


def _log_sigmoid(x):
    return jnp.minimum(x, 0.0) - jnp.log1p(jnp.exp(-jnp.abs(x)))


def _rope128(x, cosp, sina, sinb):
    return x * cosp + pltpu.roll(x, 112, 1) * sina + pltpu.roll(x, 16, 1) * sinb


W_CQ, W_CKV, W_KPE, W_FQ, W_FK, W_FV, W_MISC = 768, 256, 1024, 1024, 1024, 1024, 128
O_CQ = 0
O_CKV = O_CQ + W_CQ
O_KPE = O_CKV + W_CKV
O_FQ = O_KPE + W_KPE
O_FK = O_FQ + W_FQ
O_FV = O_FK + W_FK
O_MISC = O_FV + W_FV
W_IN_P = O_MISC + W_MISC
MISC_LF = 32


def _feat_kernel(x_ref, g1_ref, win_ref, gq_ref, gkv_ref, wuq_ref, wukv_ref, gmq_ref, gmk_ref,
                 gfq_ref, gfk_ref, b3_ref, cos_ref, sina_ref, sinb_ref, ltri_ref, pq_ref, pk_ref,
                 cq_ref, ck_ref, one_ref,
                 lat_ref, kpe_ref, fkp_ref, fvp_ref, lf_ref, qa_ref, ka_ref, va_ref, carry_ref):
    @pl.when(pl.program_id(0) == 0)
    def _():
        carry_ref[...] = jnp.zeros_like(carry_ref)

    x = x_ref[...]
    hb = _rms(x, g1_ref[...], x.shape[-1]).astype(BF16)
    cosp, sina, sinb = cos_ref[...], sina_ref[...], sinb_ref[...]

    c_q = _dot(hb, win_ref[:, O_CQ:O_CQ + W_CQ])
    c_q = _rms(c_q, gq_ref[...], Q_LORA).astype(BF16)
    q = _dot(c_q, wuq_ref[...])
    scale_m = MLA_QK ** -0.5 * LOG2E
    for h in range(MLA_HEADS):
        qh = _rms(q[:, h * HP:(h + 1) * HP], gmq_ref[...], MLA_QK)
        qh = _rope128(qh, cosp, sina, sinb) * scale_m
        qa_ref[:, h * HP:(h + 1) * HP] = qh.astype(BF16)

    c_kv = _dot(hb, win_ref[:, O_CKV:O_CKV + W_CKV])
    c_kv = _rms(c_kv, gkv_ref[...], KV_LORA)
    lat_ref[...] = c_kv
    kv = _dot(c_kv.astype(BF16), wukv_ref[...])
    kpe_t = _dot(hb, win_ref[:, O_KPE:O_KPE + W_KPE])
    off = MLA_HEADS * HP
    for h in range(MLA_HEADS):
        kh = kv[:, h * HP:(h + 1) * HP] + kpe_t[:, h * HP:(h + 1) * HP]
        kh = _rope128(_rms(kh, gmk_ref[...], MLA_QK), cosp, sina, sinb)
        ka_ref[:, h * HP:(h + 1) * HP] = kh.astype(BF16)
        va_ref[:, h * HP:(h + 1) * HP] = (kv[:, off + h * HP:off + (h + 1) * HP] + one_ref[...]).astype(BF16)

    misc = _dot(hb, win_ref[:, O_MISC:O_MISC + W_MISC])
    kpe_ref[...] = misc[:, 0:MLA_ROPE]
    logf = _log_sigmoid(misc + b3_ref[...])
    lf_ref[...] = logf[:, MISC_LF:MISC_LF + FOX_HEADS]
    lane = lax.broadcasted_iota(jnp.int32, logf.shape, 1)
    lf3 = jnp.where((lane >= MISC_LF) & (lane < MISC_LF + 3 * FOX_HEADS), logf, 0.0)
    cum = _dot_exact_rhs(ltri_ref[...], lf3) + carry_ref[...]
    carry_ref[...] = cum[cum.shape[0] - 1:, :]
    c1, c2, c3 = _split3(cum * LOG2E)
    csel = jnp.where(lane < MISC_LF + FOX_HEADS, c1, jnp.where(lane < MISC_LF + 2 * FOX_HEADS, c2, c3))
    aug_q = _dot(csel, pq_ref[...]) + cq_ref[...]
    aug_k = _dot(csel, pk_ref[...]) + ck_ref[...]

    fq = _dot(hb, win_ref[:, O_FQ:O_FQ + W_FQ])
    fk = _dot(hb, win_ref[:, O_FK:O_FK + W_FK])
    scale_f = FOX_HD ** -0.5 * LOG2E
    for h in range(FOX_HEADS):
        sl = slice(h * HP, (h + 1) * HP)
        qh = _rms(fq[:, sl], gfq_ref[...], FOX_HD) * scale_f
        qa_ref[:, off + h * HP:off + (h + 1) * HP] = (qh + aug_q[:, sl]).astype(BF16)
        kh = _rms(fk[:, sl], gfk_ref[...], FOX_HD)
        fkp_ref[:, sl] = kh
        ka_ref[:, off + h * HP:off + (h + 1) * HP] = (kh + aug_k[:, sl]).astype(BF16)
    fvp = _dot(hb, win_ref[:, O_FV:O_FV + W_FV])
    fvp_ref[...] = fvp
    for h in range(FOX_HEADS):
        sl = slice(h * HP, (h + 1) * HP)
        va_ref[:, off + h * HP:off + (h + 1) * HP] = (fvp[:, sl] + one_ref[...]).astype(BF16)


def _pad_heads(w, n_heads, width):
    k = w.shape[0]
    w = w.reshape(k, n_heads, width)
    return jnp.pad(w, ((0, 0), (0, 0), (0, HP - width))).reshape(k, n_heads * HP)


def _feat_weights(w_in, w_uq, w_ukv, g_mla_qn, g_mla_kn, g_fox_qn, g_fox_kn, b_forget):
    d = w_in.shape[0]
    o = np.cumsum([0, Q_LORA, KV_LORA, MLA_ROPE, FOX_HEADS * FOX_HD, FOX_HEADS * FOX_HD,
                   FOX_HEADS * FOX_HD, FOX_HEADS])
    w_cq, w_ckv, w_kpe, w_fq, w_fk, w_fv, w_fl = [w_in[:, o[i]:o[i + 1]] for i in range(7)]
    kpe_grp = jnp.pad(w_kpe, ((0, 0), (MLA_NOPE, HP - MLA_QK)))
    misc = jnp.concatenate([w_kpe, w_fl, w_fl, w_fl], axis=1)
    misc = jnp.pad(misc, ((0, 0), (0, W_MISC - misc.shape[1])))
    win_p = jnp.concatenate([w_cq, w_ckv, jnp.tile(kpe_grp, (1, MLA_HEADS)),
                             _pad_heads(w_fq, FOX_HEADS, FOX_HD), _pad_heads(w_fk, FOX_HEADS, FOX_HD),
                             _pad_heads(w_fv, FOX_HEADS, FOX_HD), misc], axis=1).astype(BF16)
    assert win_p.shape == (d, W_IN_P)
    wuq_p = _pad_heads(w_uq, MLA_HEADS, MLA_QK).astype(BF16)
    kvw = w_ukv.reshape(KV_LORA, MLA_HEADS, MLA_NOPE + MLA_V)
    wk = kvw[:, :, :MLA_NOPE].reshape(KV_LORA, MLA_HEADS * MLA_NOPE)
    wv = kvw[:, :, MLA_NOPE:].reshape(KV_LORA, MLA_HEADS * MLA_V)
    wukv_p = jnp.concatenate([_pad_heads(wk, MLA_HEADS, MLA_NOPE), _pad_heads(wv, MLA_HEADS, MLA_V)],
                             axis=1).astype(BF16)
    pad1 = lambda g, n: jnp.pad(g, (0, HP - n)).reshape(1, HP)
    b3 = jnp.pad(jnp.tile(b_forget, 3), (MISC_LF, W_MISC - MISC_LF - 3 * FOX_HEADS)).reshape(1, W_MISC)
    return (win_p, wuq_p, wukv_p, wk.astype(BF16), wv.astype(BF16), pad1(g_mla_qn, MLA_QK),
            pad1(g_mla_kn, MLA_QK), pad1(g_fox_qn, FOX_HD), pad1(g_fox_kn, FOX_HD), b3)


def _aug_constants():
    pq = np.zeros((W_MISC, FOX_HEADS * HP), np.float32)
    pk = np.zeros((W_MISC, FOX_HEADS * HP), np.float32)
    cq = np.zeros((1, FOX_HEADS * HP), np.float32)
    ck = np.zeros((1, FOX_HEADS * HP), np.float32)
    for h in range(FOX_HEADS):
        for j in range(3):
            pq[MISC_LF + j * FOX_HEADS + h, h * HP + FOX_HD + j] = 1.0
            ck[0, h * HP + FOX_HD + j] = 1.0
            pk[MISC_LF + j * FOX_HEADS + h, h * HP + FOX_HD + 3 + j] = -1.0
            cq[0, h * HP + FOX_HD + 3 + j] = 1.0
    return jnp.asarray(pq, BF16), jnp.asarray(pk, BF16), jnp.asarray(cq), jnp.asarray(ck)


def _rope_tables(pos):
    half = MLA_ROPE // 2
    inv = jnp.power(ROPE_THETA, -jnp.arange(half, dtype=F32) / half)
    ang = pos.astype(F32)[:, None] * inv[None, :]
    return jnp.cos(ang), jnp.sin(ang)


def _feat_call(rows, pos, tm, g_norm1, g_q_lora, g_kv_lora, fw):
    win_p, wuq_p, wukv_p, _, _, gmq, gmk, gfq, gfk, b3 = fw
    r, d = rows.shape
    cos, sin = _rope_tables(pos)
    n = cos.shape[0]
    one, zero = jnp.ones((n, MLA_NOPE), F32), jnp.zeros((n, HP - MLA_QK), F32)
    z16 = jnp.zeros_like(sin)
    cosp = jnp.concatenate([one, cos, cos, one[:, :HP - MLA_QK]], axis=1)
    sina = jnp.concatenate([zero, zero, -sin, z16, zero], axis=1)
    sinb = jnp.concatenate([zero, zero, z16, sin, zero], axis=1)
    ltri = jnp.asarray(np.tril(np.ones((tm, tm), np.float32)), BF16)
    pq, pk, cq, ck = _aug_constants()
    one = jnp.zeros((1, HP), F32).at[0, MLA_V].set(1.0)
    row = lambda w: pl.BlockSpec((tm, w), lambda i: (i, 0))
    full = lambda a: pl.BlockSpec(a.shape, lambda i: (0,) * a.ndim)
    g1 = g_norm1.reshape(1, d)
    gq = g_q_lora.reshape(1, Q_LORA)
    gkv = g_kv_lora.reshape(1, KV_LORA)
    consts = (g1, win_p, gq, gkv, wuq_p, wukv_p, gmq, gmk, gfq, gfk, b3)
    tail = (ltri, pq, pk, cq, ck, one)
    nh = MLA_HEADS + FOX_HEADS
    out_shape = (
        jax.ShapeDtypeStruct((r, KV_LORA), F32), jax.ShapeDtypeStruct((r, MLA_ROPE), F32),
        jax.ShapeDtypeStruct((r, FOX_HEADS * HP), F32), jax.ShapeDtypeStruct((r, FOX_HEADS * HP), F32),
        jax.ShapeDtypeStruct((r, FOX_HEADS), F32), jax.ShapeDtypeStruct((r, nh * HP), BF16),
        jax.ShapeDtypeStruct((r, nh * HP), BF16), jax.ShapeDtypeStruct((r, nh * HP), BF16))
    return pl.pallas_call(
        _feat_kernel,
        grid=(r // tm,),
        in_specs=[row(d)] + [full(a) for a in consts] + [row(HP), row(HP), row(HP)] + [full(a) for a in tail],
        out_specs=[row(s.shape[1]) for s in out_shape],
        out_shape=out_shape,
        scratch_shapes=[pltpu.VMEM((1, W_MISC), F32)],
        compiler_params=pltpu.CompilerParams(dimension_semantics=("arbitrary",),
                                             vmem_limit_bytes=VMEM_LIMIT),
        name="feat",
    )(rows, *consts, cosp, sina, sinb, *tail)


def _attn_kernel(q_ref, k_ref, v_ref, o_ref, m_ref, acc_ref, *, tq, tk):
    assert tq == tk
    qi = pl.program_id(1)
    qs = [q_ref[:, j * HP:(j + 1) * HP] for j in range(2)]

    def block(kb, carry, masked):
        start = pl.multiple_of(kb * tk, tk)
        new = []
        for j in range(2):
            m, acc = carry[2 * j], carry[2 * j + 1]
            k = k_ref[pl.ds(start, tk), j * HP:(j + 1) * HP]
            v = v_ref[pl.ds(start, tk), j * HP:(j + 1) * HP]
            s = _dot_nt(qs[j], k)
            if masked:
                rowi = qi * tq + lax.broadcasted_iota(jnp.int32, s.shape, 0)
                coli = kb * tk + lax.broadcasted_iota(jnp.int32, s.shape, 1)
                s = jnp.where(coli <= rowi, s, NEG_INF)
            m_new = jnp.maximum(m, jnp.max(s, axis=-1, keepdims=True))
            alpha = jnp.exp2(m - m_new)
            p = jnp.exp2((s - m_new).astype(BF16))
            new += [m_new, alpha * acc + _dot(p, v)]
        return tuple(new)

    def save(carry):
        for j in range(2):
            m_ref[j] = carry[2 * j]
            acc_ref[j] = carry[2 * j + 1]

    def load():
        return (m_ref[0], acc_ref[0], m_ref[1], acc_ref[1])

    init = (jnp.full((tq, 1), NEG_INF, F32), jnp.zeros((tq, HP), F32)) * 2
    save(lax.fori_loop(0, qi // 2, lambda i, c: block(2 * i + 1, block(2 * i, c, False), False), init))

    @pl.when(qi % 2 == 1)
    def _():
        save(block(qi - 1, load(), False))

    carry = block(qi, load(), True)
    outs = [carry[2 * j + 1] / carry[2 * j + 1][:, MLA_V:MLA_V + 1] for j in range(2)]
    lane = lax.broadcasted_iota(jnp.int32, outs[0].shape, 1)
    o_ref[...] = jnp.where(lane < MLA_V, outs[0], pltpu.roll(outs[1], MLA_V, 1))


def _attn_call(qa, ka, va, tp, tq, tk):
    npairs = qa.shape[1] // (2 * HP)
    return pl.pallas_call(
        functools.partial(_attn_kernel, tq=tq, tk=tk),
        grid=(npairs, tp // tq),
        in_specs=[pl.BlockSpec((tq, 2 * HP), lambda p, i: (i, p)),
                  pl.BlockSpec((tp, 2 * HP), lambda p, i: (0, p)),
                  pl.BlockSpec((tp, 2 * HP), lambda p, i: (0, p))],
        out_specs=pl.BlockSpec((tq, 2 * MLA_V), lambda p, i: (i, p)),
        out_shape=jax.ShapeDtypeStruct((tp, npairs * 2 * MLA_V), F32),
        scratch_shapes=[pltpu.VMEM((2, tq, 1), F32), pltpu.VMEM((2, tq, HP), F32)],
        compiler_params=pltpu.CompilerParams(dimension_semantics=("arbitrary", "arbitrary"),
                                             vmem_limit_bytes=VMEM_LIMIT),
        name="attn",
    )(qa, ka, va)


def _softmax_step(s, pv, m_ref, l_ref, acc_ref):
    m_old = m_ref[...]
    m_new = jnp.maximum(m_old, jnp.max(s, axis=-1, keepdims=True))
    alpha = jnp.exp2(m_old - m_new)
    p = jnp.exp2(s - m_new)
    l_ref[...] = alpha * l_ref[...] + jnp.sum(p, axis=-1, keepdims=True)
    acc_ref[...] = alpha * acc_ref[...] + pv(p.astype(BF16))
    m_ref[...] = m_new


def _head_diag(x, hm):
    ns = x.shape[0] // MLA_HEADS
    return jnp.sum(x.reshape(ns, MLA_HEADS, x.shape[1]) * hm[None], axis=1)


def _decode_kernel(pt_ref, *refs, pg):
    pools = refs[:5]
    (cos_ref, sin_ref, qbm_ref, qbf_ref, qn_ref, qr_ref, qf_ref, kan_ref, latn_ref, fvn_ref, lfn_ref,
     wk_ref, wv_ref, gn_ref, gr_ref, bexp_ref, lstrict_ref,
     ltri8_ref, es_ref, fh_ref, hm_ref, causal_ref,
     o_ref, mm_ref, lm_ref, am_ref, mf_ref, lfs_ref, af_ref, carry_ref, cnq_ref, qng_ref,
     lat_buf, rope_buf, fk_buf, fv_buf, lf_buf, sems) = refs[5:]
    bufs = (lat_buf, rope_buf, fk_buf, fv_buf, lf_buf)
    b, g = pl.program_id(0), pl.program_id(1)
    nch = pl.num_programs(1)
    step = b * nch + g
    slot = step % 2
    nm = MLA_HEADS * HP

    def page_copies(bb, gg, sl):
        first = bb * (nch * pg) + (nch - 1 - gg) * pg
        out = []
        for i in range(pg):
            pid = pt_ref[first + i]
            tok = pl.ds(i * LANES, LANES)
            dsts = (lat_buf.at[sl, tok, :], rope_buf.at[sl, :, tok], fk_buf.at[sl, :, tok],
                    fv_buf.at[sl, :, tok], lf_buf.at[sl, pl.ds(i * FOX_HEADS, FOX_HEADS), :])
            out += [pltpu.make_async_copy(pool.at[pid], dst, sems.at[sl, j])
                    for j, (pool, dst) in enumerate(zip(pools, dsts))]
        return out

    @pl.when(step == 0)
    def _():
        for cp in page_copies(b, g, slot):
            cp.start()

    @pl.when(step + 1 < pl.num_programs(0) * nch)
    def _():
        wrap = g + 1 == nch
        for cp in page_copies(jnp.where(wrap, b + 1, b), jnp.where(wrap, 0, g + 1), 1 - slot):
            cp.start()

    @pl.when(g == 0)
    def _():
        causal = causal_ref[...]
        ka = kan_ref[0]
        s_m = jnp.where(causal > 0, _dot_nt(qbm_ref[0], ka[:, :nm]), NEG_INF)
        mm_ref[...] = jnp.full_like(mm_ref, NEG_INF)
        lm_ref[...] = jnp.zeros_like(lm_ref)
        am_ref[...] = jnp.zeros_like(am_ref)
        lat_n = latn_ref[0].astype(BF16)
        _softmax_step(s_m, lambda p: _dot(p, lat_n), mm_ref, lm_ref, am_ref)
        cn = _dot_exact_rhs(ltri8_ref[...], lfn_ref[0])
        cnq = jnp.sum(_dot_exact_rhs(es_ref[...], cn) * fh_ref[...].astype(F32), axis=-1, keepdims=True)
        a, b, c = _split3(cn)
        fh = fh_ref[...]
        cnk = _dot_nt(fh, a) + _dot_nt(fh, b) + _dot_nt(fh, c)
        s_f = _dot_nt(qbf_ref[0], ka[:, nm:]) + (cnq - cnk) * LOG2E
        s_f = jnp.where(causal > 0, s_f, NEG_INF)
        mf_ref[...] = jnp.full_like(mf_ref, NEG_INF)
        lfs_ref[...] = jnp.zeros_like(lfs_ref)
        af_ref[...] = jnp.zeros_like(af_ref)
        fv_n = fvn_ref[0].astype(BF16)
        _softmax_step(s_f, lambda p: _dot(p, fv_n), mf_ref, lfs_ref, af_ref)
        cnq_ref[...] = cnq * LOG2E
        carry_ref[...] = jnp.zeros_like(carry_ref)
        qng_ref[...] = (qn_ref[0].astype(F32) * gn_ref[...]).astype(BF16)

    for cp in page_copies(b, g, slot):
        cp.wait()

    half = MLA_ROPE // 2
    nrep = qf_ref.shape[1] // FOX_HEADS
    lat = lat_buf[slot].astype(BF16)
    kn = _dot(lat, wk_ref[...])
    kpe = rope_buf[slot]
    ssq = _dot_nt(bexp_ref[...], (kn * kn).astype(BF16)) + jnp.sum(kpe * kpe, axis=0, keepdims=True)
    r_t = lax.rsqrt(ssq * (1.0 / MLA_QK) + EPS)
    kg = kpe * gr_ref[...]
    sw = jnp.concatenate([kg[half:], kg[:half]], axis=0)
    kr = kg * cos_ref[...] + sw * sin_ref[...]
    s_m = (_dot_nt(qng_ref[...], kn.astype(BF16)) + _dot(qr_ref[0], kr.astype(BF16))) * r_t
    _softmax_step(s_m, lambda p: _dot(p, lat), mm_ref, lm_ref, am_ref)
    lf = lf_buf[slot]
    a, b, c = _split3(lf)
    lstrict = lstrict_ref[...]
    within = _dot(a, lstrict) + _dot(b, lstrict) + _dot(c, lstrict)
    tot = jnp.sum(lf, axis=-1, keepdims=True)
    run = carry_ref[...]
    decs = [None] * pg
    for i in reversed(range(pg)):
        rows = slice(i * FOX_HEADS, (i + 1) * FOX_HEADS)
        decs[i] = jnp.concatenate([within[rows] + run] * nrep, axis=0)
        run = run + tot[rows]
    carry_ref[...] = run
    dec = jnp.concatenate(decs, axis=1) * LOG2E + cnq_ref[...]
    s_f = _dot(qf_ref[0], fk_buf[slot].astype(BF16)) + dec
    fv = fv_buf[slot].astype(BF16)
    _softmax_step(s_f, lambda p: _dot_nt(p, fv), mf_ref, lfs_ref, af_ref)

    @pl.when(g == pl.num_programs(1) - 1)
    def _():
        hm = hm_ref[...]
        nv = MLA_HEADS * MLA_V
        o_lat = (am_ref[...] / lm_ref[...]).astype(BF16)
        o_ref[0, :, 0:nv] = _head_diag(_dot(o_lat, wv_ref[...]), hm)
        o_ref[0, :, nv:2 * nv] = _head_diag(af_ref[...] / lfs_ref[...], hm)


def _decode_constants(ns):
    nr = ns * MLA_HEADS
    hh = np.arange(nr) % MLA_HEADS
    ss = np.arange(nr) // MLA_HEADS
    bexp = (np.arange(MLA_HEADS * MLA_NOPE)[None, :] // MLA_NOPE == hh[:, None]).astype(np.float32)
    lstrict = np.tril(np.ones((LANES, LANES), np.float32), -1)
    ltri8 = np.tril(np.ones((ns, ns), np.float32))
    es = (np.arange(ns)[None, :] == ss[:, None]).astype(np.float32)
    fh = (np.arange(FOX_HEADS)[None, :] == hh[:, None]).astype(np.float32)
    hm = (np.arange(MLA_HEADS * MLA_V)[None, :] // MLA_V == np.arange(MLA_HEADS)[:, None]).astype(np.float32)
    causal = (np.arange(ns)[None, :] <= ss[:, None]).astype(np.float32)
    bf = lambda a: jnp.asarray(a, BF16)
    return (bf(bexp), bf(lstrict), bf(ltri8), bf(es), bf(fh), jnp.asarray(hm), jnp.asarray(causal))


def _block_diag_rows(q, width):
    b, s, h, w = q.shape
    eye = jnp.eye(h, dtype=q.dtype)
    return (q[:, :, :, None, :] * eye[None, None, :, :, None]).reshape(b, s * h, h * w)


def _decode_call(qa_s, ka_s, lat_s, fv_s, lf_s, pools, page_table, fw, g_mla_kn, pg):
    lat_pool, rope_pool, fk_pool, fv_pool, lf_pool = pools
    nb, npages = page_table.shape
    ns = qa_s.shape[0] // nb
    nr = ns * MLA_HEADS
    nch = npages // pg
    page = lat_pool.shape[1]
    assert page == LANES and npages % pg == 0
    rope_pool = jnp.swapaxes(rope_pool, 1, 2)
    fk_pool = jnp.transpose(fk_pool, (0, 2, 3, 1)).reshape(fk_pool.shape[0], -1, page)
    fv_pool = jnp.transpose(fv_pool, (0, 2, 3, 1)).reshape(fv_pool.shape[0], -1, page)
    lf_pool = jnp.swapaxes(lf_pool, 1, 2)
    past = npages * page
    nm = MLA_HEADS * HP
    _, _, _, wk, wv, _, _, _, _, _ = fw
    q4 = qa_s.reshape(nb, ns, MLA_HEADS + FOX_HEADS, HP)
    qm, qfx = q4[:, :, :MLA_HEADS], q4[:, :, MLA_HEADS:, :FOX_HD]
    qbm = _block_diag_rows(qm, HP)
    qbf = _block_diag_rows(jnp.pad(qfx, ((0, 0), (0, 0), (0, 0), (0, HP - FOX_HD))), HP)
    qn = _block_diag_rows(qm[..., :MLA_NOPE], MLA_NOPE)
    qr = qm[..., MLA_NOPE:MLA_QK].reshape(nb, nr, MLA_ROPE)
    qf = _block_diag_rows(qfx, FOX_HD)
    kan = ka_s.reshape(nb, ns, ka_s.shape[1])
    latn = lat_s.reshape(nb, ns, lat_s.shape[1])
    fvn = fv_s.reshape(nb, ns, -1)
    lfn = lf_s.reshape(nb, ns, FOX_HEADS)
    cos, sin = _rope_tables(jnp.arange(past, dtype=jnp.int32))
    cos2 = jnp.concatenate([cos, cos], axis=1).T
    sin2 = jnp.concatenate([-sin, sin], axis=1).T
    gn = jnp.tile(g_mla_kn[:MLA_NOPE], MLA_HEADS).reshape(1, MLA_HEADS * MLA_NOPE)
    gr = jnp.broadcast_to(g_mla_kn[MLA_NOPE:MLA_QK].reshape(MLA_ROPE, 1), (MLA_ROPE, pg * page))
    consts = (wk, wv, gn, gr) + _decode_constants(ns)
    pt_flat = page_table.reshape(-1).astype(jnp.int32)

    page_args = (lat_pool, rope_pool, fk_pool, fv_pool, lf_pool)
    page_specs = [pl.BlockSpec(memory_space=pl.ANY)] * len(page_args)
    t = pg * page
    page_bufs = [pltpu.VMEM((2, t, lat_pool.shape[2]), F32), pltpu.VMEM((2, rope_pool.shape[1], t), F32),
                 pltpu.VMEM((2, fk_pool.shape[1], t), F32), pltpu.VMEM((2, fv_pool.shape[1], t), F32),
                 pltpu.VMEM((2, pg * lf_pool.shape[1], page), F32),
                 pltpu.SemaphoreType.DMA((2, len(page_args)))]
    tab = pl.BlockSpec((MLA_ROPE, pg * page), lambda b, g, pt: (0, nch - 1 - g))
    seq = lambda a: pl.BlockSpec((1,) + a.shape[1:], lambda b, g, pt: (b,) + (0,) * (a.ndim - 1))
    full = lambda a: pl.BlockSpec(a.shape, lambda b, g, pt: (0,) * a.ndim)
    per_seq = (qbm, qbf, qn, qr, qf, kan, latn, fvn, lfn)
    nv = MLA_HEADS * MLA_V
    grid_spec = pltpu.PrefetchScalarGridSpec(
        num_scalar_prefetch=1,
        grid=(nb, nch),
        in_specs=page_specs + [tab, tab] + [seq(a) for a in per_seq] + [full(a) for a in consts],
        out_specs=pl.BlockSpec((1, ns, 2 * nv), lambda b, g, pt: (b, 0, 0)),
        scratch_shapes=[pltpu.VMEM((nr, 1), F32), pltpu.VMEM((nr, 1), F32), pltpu.VMEM((nr, KV_LORA), F32),
                        pltpu.VMEM((nr, 1), F32), pltpu.VMEM((nr, 1), F32), pltpu.VMEM((nr, nv), F32),
                        pltpu.VMEM((FOX_HEADS, 1), F32), pltpu.VMEM((nr, 1), F32),
                        pltpu.VMEM((nr, MLA_HEADS * MLA_NOPE), BF16)] + page_bufs)
    out = pl.pallas_call(
        functools.partial(_decode_kernel, pg=pg),
        grid_spec=grid_spec,
        out_shape=jax.ShapeDtypeStruct((nb, ns, 2 * nv), F32),
        compiler_params=pltpu.CompilerParams(dimension_semantics=("arbitrary", "arbitrary"),
                                             vmem_limit_bytes=VMEM_LIMIT),
        name="decode",
    )(pt_flat, *page_args, cos2, sin2, *per_seq, *consts)
    return out.reshape(nb * ns, 2 * nv)


def _merge_kernel(x_ref, o_ref, gmo_ref, gfo_ref, wout_ref, g2_ref, wpq_ref, ka_ref, kb_ref,
                  x1_ref, h2_ref, sa_ref, sb_ref):
    o = o_ref[...]
    nv = MLA_HEADS * MLA_V
    mixed = jnp.concatenate([_rms(o[:, :nv], gmo_ref[...], nv).astype(BF16),
                             _rms(o[:, nv:], gfo_ref[...], o.shape[1] - nv).astype(BF16)], axis=1)
    x1 = x_ref[...] + _dot(mixed, wout_ref[...])
    x1_ref[...] = x1
    h2 = _rms(x1, g2_ref[...], x1.shape[1]).astype(BF16)
    h2_ref[...] = h2
    pq = _dot(h2, wpq_ref[...])
    half = PEER_DK // 2
    for h in range(PEER_HEADS):
        qa = pq[:, h * PEER_DK:h * PEER_DK + half].astype(BF16)
        qb = pq[:, h * PEER_DK + half:(h + 1) * PEER_DK].astype(BF16)
        sa_ref[h * PEER_NKEYS:(h + 1) * PEER_NKEYS, :] = _dot_nt(ka_ref[h], qa)
        sb_ref[h * PEER_NKEYS:(h + 1) * PEER_NKEYS, :] = _dot_nt(kb_ref[h], qb)


def _merge_call(x, o, tm, g_mla_out, g_fox_out, w_out, g_norm2, w_peer_query, keys_a, keys_b):
    r, d = x.shape
    row = lambda w: pl.BlockSpec((tm, w), lambda i: (i, 0))
    full = lambda a: pl.BlockSpec(a.shape, lambda i: (0,) * a.ndim)
    col = pl.BlockSpec((PEER_HEADS * PEER_NKEYS, tm), lambda i: (0, i))
    consts = (g_mla_out.reshape(1, -1), g_fox_out.reshape(1, -1), w_out.astype(BF16),
              g_norm2.reshape(1, -1), w_peer_query.astype(BF16), keys_a.astype(BF16), keys_b.astype(BF16))
    nk = PEER_HEADS * PEER_NKEYS
    return pl.pallas_call(
        _merge_kernel,
        grid=(r // tm,),
        in_specs=[row(d), row(o.shape[1])] + [full(a) for a in consts],
        out_specs=[row(d), row(d), col, col],
        out_shape=(jax.ShapeDtypeStruct((r, d), F32), jax.ShapeDtypeStruct((r, d), BF16),
                   jax.ShapeDtypeStruct((nk, r), F32), jax.ShapeDtypeStruct((nk, r), F32)),
        compiler_params=pltpu.CompilerParams(dimension_semantics=("arbitrary",),
                                             vmem_limit_bytes=VMEM_LIMIT),
        name="merge",
    )(x, o, *consts)


_CAND = [(i, j) for i in range(PEER_TOPK) for j in range(PEER_TOPK) if (i + 1) * (j + 1) <= PEER_TOPK]
_NCAND = -(-len(_CAND) // 8) * 8


def _extract_max(cur, order, first_only):
    m = jnp.max(cur, axis=0, keepdims=True)
    if not first_only:
        return m, cur == m
    first = jnp.min(jnp.where(cur == m, order, 1e9), axis=0, keepdims=True)
    return m, order == first


def _topk_tables(sa, sb, ci, cj, first_only):
    k = PEER_TOPK
    tn = sa.shape[1]
    key_order = lax.broadcasted_iota(jnp.int32, (PEER_NKEYS, tn), 0).astype(F32)
    cand_order = lax.broadcasted_iota(jnp.int32, (_NCAND, tn), 0).astype(F32)
    ranks, vals = [], []
    for s in (sa, sb):
        cur = s
        rank = jnp.full(s.shape, float(k), F32)
        v = []
        for i in range(k):
            m, hot = _extract_max(cur, key_order, first_only)
            v.append(m)
            rank = jnp.where(hot, float(i), rank)
            cur = jnp.where(hot, -jnp.inf, cur)
        ranks.append(rank)
        vals.append(v)
    rank_a, rank_b = ranks
    va, vb = vals
    cand = jnp.full((_NCAND, tn), -jnp.inf, F32)
    for i in range(k):
        cand = jnp.where(ci == float(i), va[i], cand)
    cur = jnp.full((_NCAND, tn), -jnp.inf, F32)
    for j in range(k):
        cur = jnp.where(cj == float(j), cand + vb[j], cur)
    top = va[0] + vb[0]
    z = jnp.zeros((1, tn), F32)
    sel = jnp.zeros((_NCAND, tn), F32)
    for _ in range(k):
        m, hot = _extract_max(cur, cand_order, first_only)
        z = z + jnp.exp(m - top)
        sel = jnp.where(hot, 1.0, sel)
        cur = jnp.where(hot, -jnp.inf, cur)
    ca = jnp.zeros(sa.shape, F32)
    for i in range(k):
        c_i = jnp.sum(jnp.where(ci == float(i), sel, 0.0), axis=0, keepdims=True)
        ca = jnp.where(rank_a == float(i), c_i, ca)
    count = lambda hit: jnp.sum(jnp.where(hit, 1.0, 0.0), axis=0, keepdims=True)
    taken = jnp.maximum(jnp.maximum(count(rank_a < k), count(rank_b < k)), count(sel > 0))
    return rank_b, ca, jnp.exp(sa - va[0]) / z, jnp.exp(sb - vb[0]), taken


def _topk_kernel(sa_ref, sb_ref, ci_ref, cj_ref, rb_ref, ca_ref, ea_ref, eb_ref):
    def run(first_only):
        rank_b, ca, ea, eb, taken = _topk_tables(sa_ref[...], sb_ref[...], ci_ref[...], cj_ref[...],
                                                 first_only)
        rb_ref[...] = rank_b.astype(rb_ref.dtype)
        ca_ref[...] = ca
        ea_ref[...] = ea
        eb_ref[...] = eb.astype(eb_ref.dtype)
        return taken

    taken = run(False)

    @pl.when(jnp.max(taken) > PEER_TOPK)
    def _():
        run(True)


def _topk_call(sa, sb, tn):
    nk, r = sa.shape
    ci = np.full((_NCAND, tn), -1.0, np.float32)
    cj = np.full((_NCAND, tn), -1.0, np.float32)
    for n, (i, j) in enumerate(_CAND):
        ci[n, :], cj[n, :] = i, j
    col = pl.BlockSpec((PEER_NKEYS, tn), lambda i, h: (h, i))
    full = pl.BlockSpec((_NCAND, tn), lambda i, h: (0, 0))
    return pl.pallas_call(
        _topk_kernel,
        grid=(r // tn, nk // PEER_NKEYS),
        in_specs=[col, col, full, full],
        out_specs=[col] * 4,
        out_shape=tuple(jax.ShapeDtypeStruct((nk, r), dt) for dt in (BF16, F32, F32, BF16)),
        compiler_params=pltpu.CompilerParams(dimension_semantics=("arbitrary", "arbitrary"),
                                             vmem_limit_bytes=VMEM_LIMIT),
        name="topk",
    )(sa, sb, jnp.asarray(ci), jnp.asarray(cj))


def _peer_kernel(h2_ref, x1_ref, u_ref, vt_ref, rb_ref, ca_ref, ea_ref, eb_ref, y_ref, acc_ref, w_ref,
                 *, a_per_step):
    c = pl.program_id(1)

    @pl.when(c == 0)
    def _():
        acc_ref[...] = jnp.zeros_like(acc_ref)

    s = _dot_nt(u_ref[...], h2_ref[...])
    act = 0.5 * s * (1.0 + lax.erf(s * np.float32(np.sqrt(0.5))))
    zero = jnp.zeros((), BF16)
    for aa in range(a_per_step):
        gate = jnp.zeros((PEER_NKEYS, s.shape[1]), BF16)
        for h in range(PEER_HEADS):
            row = pl.ds(h * PEER_NKEYS + c * a_per_step + aa, 1)
            sl = slice(h * PEER_NKEYS, (h + 1) * PEER_NKEYS)
            c_a = ca_ref[row, :].astype(BF16)
            e_a = ea_ref[row, :].astype(BF16)
            gate = gate + jnp.where(rb_ref[sl, :] < c_a, eb_ref[sl, :], zero) * e_a
        blk = slice(aa * PEER_NKEYS, (aa + 1) * PEER_NKEYS)
        w_ref[blk, :] = gate * act[blk, :].astype(BF16)
    acc_ref[...] += _dot(vt_ref[...], w_ref[...])

    @pl.when(c == pl.num_programs(1) - 1)
    def _():
        y_ref[...] = x1_ref[...] + acc_ref[...].T


def _peer_call(h2, x1, u, vt, tables, tm, a_per_step):
    r, d = h2.shape
    ne = u.shape[0]
    ec = a_per_step * PEER_NKEYS
    nk = PEER_HEADS * PEER_NKEYS
    row = pl.BlockSpec((tm, d), lambda i, c: (i, 0))
    col = pl.BlockSpec((nk, tm), lambda i, c: (0, i))
    return pl.pallas_call(
        functools.partial(_peer_kernel, a_per_step=a_per_step),
        grid=(r // tm, ne // ec),
        in_specs=[row, row, pl.BlockSpec((ec, d), lambda i, c: (c, 0)),
                  pl.BlockSpec((d, ec), lambda i, c: (0, c)), col, col, col, col],
        out_specs=row,
        out_shape=jax.ShapeDtypeStruct((r, d), F32),
        scratch_shapes=[pltpu.VMEM((d, tm), F32), pltpu.VMEM((ec, tm), BF16)],
        compiler_params=pltpu.CompilerParams(dimension_semantics=("arbitrary", "arbitrary"),
                                             vmem_limit_bytes=VMEM_LIMIT),
        name="peer",
    )(h2, x1, u, vt, *tables)


def _round_up(n, m):
    return -(-n // m) * m


def _layer(xp, xs, pools, page_table, p, tiles):
    t, d = xp.shape
    nb, ns, _ = xs.shape
    past = page_table.shape[1] * pools[0].shape[1]
    tp = _round_up(t, tiles["tq"])
    n_s = nb * ns
    r = tp + n_s
    assert r % tiles["tm"] == 0 and r % tiles["tp"] == 0
    rows = jnp.concatenate([xp, jnp.zeros((tp - t, d), xp.dtype), xs.reshape(n_s, d)], axis=0)
    pos = jnp.concatenate([jnp.arange(tp, dtype=jnp.int32),
                           past + (jnp.arange(n_s, dtype=jnp.int32) % ns)])
    fw = _feat_weights(p['w_in'], p['w_uq'], p['w_ukv'], p['g_mla_qn'], p['g_mla_kn'],
                       p['g_fox_qn'], p['g_fox_kn'], p['b_forget'])
    lat, kpe, fkp, fvp, lf, qa, ka, va = _feat_call(rows, pos, tiles["tm"], p['g_norm1'], p['g_q_lora'],
                                                    p['g_kv_lora'], fw)
    fk = fkp.reshape(r, FOX_HEADS, HP)[:, :, :FOX_HD]
    fvh = fvp.reshape(r, FOX_HEADS, HP)[:, :, :FOX_HD]
    o_p = _attn_call(qa, ka, va, tp, tiles["tq"], tiles["tk"])
    o_s = _decode_call(qa[tp:], ka[tp:], lat[tp:], fvh[tp:], lf[tp:], pools, page_table, fw, p['g_mla_kn'],
                       tiles["pg"])
    o = jnp.concatenate([o_p, o_s], axis=0)
    x1, h2, sa, sb = _merge_call(rows, o, tiles["tm"], p['g_mla_out'], p['g_fox_out'], p['w_out'],
                                 p['g_norm2'], p['w_peer_query'], p['peer_keys_a'], p['peer_keys_b'])
    tables = _topk_call(sa, sb, tiles["tn"])
    y = _peer_call(h2, x1, p['peer_u'].astype(BF16), p['peer_v'].astype(BF16).T, tables,
                   tiles["tp"], tiles["ap"])
    prm = lambda a: a[:t][None]
    smp = lambda a: a[tp:].reshape((nb, ns) + a.shape[1:])
    outs = (lat, kpe, fk, fvh, lf)
    return (y[:t], y[tp:].reshape(nb, ns, d), tuple(prm(a) for a in outs), tuple(smp(a) for a in outs))


_TILES = dict(tm=256, tq=512, tk=512, pg=16, tn=256, tp=512, ap=8)


def kernel(x_prompt, x_sample, cache_mla_latent, cache_mla_rope, cache_fox_k, cache_fox_v, cache_fox_logf,
           page_table, meta_tokens, g_norm1, w_in, g_q_lora, g_kv_lora, w_uq, w_ukv, g_mla_qn, g_mla_kn,
           g_fox_qn, g_fox_kn, b_forget, g_mla_out, g_fox_out, w_out, g_norm2, w_peer_query, peer_keys_a,
           peer_keys_b, peer_u, peer_v):
    tiles = _TILES
    depth = w_in.shape[0]
    assert depth == 1 and x_prompt.shape[0] == 1
    xp = jnp.concatenate([meta_tokens.astype(x_prompt.dtype), x_prompt[0]], axis=0)
    xs = x_sample
    p_rows, s_rows = [], []
    for l in range(depth):
        p = {
            'g_norm1': g_norm1[l], 'w_in': w_in[l], 'g_q_lora': g_q_lora[l], 'g_kv_lora': g_kv_lora[l],
            'w_uq': w_uq[l], 'w_ukv': w_ukv[l], 'g_mla_qn': g_mla_qn[l], 'g_mla_kn': g_mla_kn[l],
            'g_fox_qn': g_fox_qn[l], 'g_fox_kn': g_fox_kn[l], 'b_forget': b_forget[l],
            'g_mla_out': g_mla_out[l], 'g_fox_out': g_fox_out[l], 'w_out': w_out[l],
            'g_norm2': g_norm2[l], 'w_peer_query': w_peer_query[l], 'peer_keys_a': peer_keys_a[l],
            'peer_keys_b': peer_keys_b[l], 'peer_u': peer_u[l], 'peer_v': peer_v[l],
        }
        pools = (cache_mla_latent[l], cache_mla_rope[l], cache_fox_k[l], cache_fox_v[l], cache_fox_logf[l])
        yp, ys, rp, rs = _layer(xp, xs, pools, page_table, p, tiles)
        xp, xs = yp, ys
        p_rows.append(rp)
        s_rows.append(rs)
    p_lat, p_rope, p_fk, p_fv, p_lf = [jnp.stack(r) for r in zip(*p_rows)]
    s_lat, s_rope, s_fk, s_fv, s_lf = [jnp.stack(r) for r in zip(*s_rows)]
    return (xp[NUM_META:][None], xs, p_lat, p_rope, p_fk, p_fv, p_lf, s_lat, s_rope, s_fk, s_fv, s_lf)
```

```python
import functools

import numpy as np
import jax
import jax.numpy as jnp
from jax import lax
from jax.experimental import pallas as pl
from jax.experimental.pallas import tpu as pltpu

NUM_META = 16
MLA_HEADS = 8
MLA_NOPE = 64
MLA_ROPE = 32
MLA_QK = MLA_NOPE + MLA_ROPE
MLA_V = 64
Q_LORA = 768
KV_LORA = 256
FOX_HEADS = 8
FOX_HD = 64
PEER_HEADS = 8
PEER_NKEYS = 128
PEER_DK = 256
PEER_TOPK = 16
ROPE_THETA = 10000.0
EPS = 1e-6
NEG_INF = -1e30
LOG2E = float(np.log2(np.e))

LANES = 128
HP = 128
VMEM_LIMIT = 56 * 1024 * 1024

F32 = jnp.float32
BF16 = jnp.bfloat16


def _dot(a, b):
    return jnp.dot(a, b, preferred_element_type=F32)


def _dot_nt(a, b):
    return lax.dot_general(a, b, (((1,), (1,)), ((), ())), preferred_element_type=F32)


def _split2(x):
    hi = x.astype(BF16)
    lo = (x - hi.astype(F32)).astype(BF16)
    return hi, lo


def _split3(x):
    a = x.astype(BF16)
    r = x - a.astype(F32)
    b = r.astype(BF16)
    c = (r - b.astype(F32)).astype(BF16)
    return a, b, c


def _dot_exact_rhs(m, x):
    a, b, c = _split3(x)
    return _dot(m, a) + _dot(m, b) + _dot(m, c)


def _rms(x, g, n):
    ss = jnp.sum(x * x, axis=-1, keepdims=True)
    return x * lax.rsqrt(ss * (1.0 / n) + EPS) * g


def _log_sigmoid(x):
    return jnp.minimum(x, 0.0) - jnp.log1p(jnp.exp(-jnp.abs(x)))


def _rope128(x, cosp, sina, sinb):
    return x * cosp + pltpu.roll(x, 112, 1) * sina + pltpu.roll(x, 16, 1) * sinb


W_CQ, W_CKV, W_KPE, W_FQ, W_FK, W_FV, W_MISC = 768, 256, 1024, 1024, 1024, 1024, 128
O_CQ = 0
O_CKV = O_CQ + W_CQ
O_KPE = O_CKV + W_CKV
O_FQ = O_KPE + W_KPE
O_FK = O_FQ + W_FQ
O_FV = O_FK + W_FK
O_MISC = O_FV + W_FV
W_IN_P = O_MISC + W_MISC
MISC_LF = 32


def _feat_kernel(x_ref, g1_ref, win_ref, gq_ref, gkv_ref, wuq_ref, wukv_ref, gmq_ref, gmk_ref,
                 gfq_ref, gfk_ref, b3_ref, cos_ref, sina_ref, sinb_ref, ltri_ref, pq_ref, pk_ref,
                 cq_ref, ck_ref, one_ref,
                 lat_ref, kpe_ref, fkp_ref, fvp_ref, lf_ref, qa_ref, ka_ref, va_ref, carry_ref):
    @pl.when(pl.program_id(0) == 0)
    def _():
        carry_ref[...] = jnp.zeros_like(carry_ref)

    x = x_ref[...]
    hb = _rms(x, g1_ref[...], x.shape[-1]).astype(BF16)
    cosp, sina, sinb = cos_ref[...], sina_ref[...], sinb_ref[...]

    c_q = _dot(hb, win_ref[:, O_CQ:O_CQ + W_CQ])
    c_q = _rms(c_q, gq_ref[...], Q_LORA).astype(BF16)
    q = _dot(c_q, wuq_ref[...])
    scale_m = MLA_QK ** -0.5 * LOG2E
    for h in range(MLA_HEADS):
        qh = _rms(q[:, h * HP:(h + 1) * HP], gmq_ref[...], MLA_QK)
        qh = _rope128(qh, cosp, sina, sinb) * scale_m
        qa_ref[:, h * HP:(h + 1) * HP] = qh.astype(BF16)

    c_kv = _dot(hb, win_ref[:, O_CKV:O_CKV + W_CKV])
    c_kv = _rms(c_kv, gkv_ref[...], KV_LORA)
    lat_ref[...] = c_kv
    kv = _dot(c_kv.astype(BF16), wukv_ref[...])
    kpe_t = _dot(hb, win_ref[:, O_KPE:O_KPE + W_KPE])
    off = MLA_HEADS * HP
    for h in range(MLA_HEADS):
        kh = kv[:, h * HP:(h + 1) * HP] + kpe_t[:, h * HP:(h + 1) * HP]
        kh = _rope128(_rms(kh, gmk_ref[...], MLA_QK), cosp, sina, sinb)
        ka_ref[:, h * HP:(h + 1) * HP] = kh.astype(BF16)
        va_ref[:, h * HP:(h + 1) * HP] = (kv[:, off + h * HP:off + (h + 1) * HP] + one_ref[...]).astype(BF16)

    misc = _dot(hb, win_ref[:, O_MISC:O_MISC + W_MISC])
    kpe_ref[...] = misc[:, 0:MLA_ROPE]
    logf = _log_sigmoid(misc + b3_ref[...])
    lf_ref[...] = logf[:, MISC_LF:MISC_LF + FOX_HEADS]
    lane = lax.broadcasted_iota(jnp.int32, logf.shape, 1)
    lf3 = jnp.where((lane >= MISC_LF) & (lane < MISC_LF + 3 * FOX_HEADS), logf, 0.0)
    cum = _dot_exact_rhs(ltri_ref[...], lf3) + carry_ref[...]
    carry_ref[...] = cum[cum.shape[0] - 1:, :]
    c1, c2, c3 = _split3(cum * LOG2E)
    csel = jnp.where(lane < MISC_LF + FOX_HEADS, c1, jnp.where(lane < MISC_LF + 2 * FOX_HEADS, c2, c3))
    aug_q = _dot(csel, pq_ref[...]) + cq_ref[...]
    aug_k = _dot(csel, pk_ref[...]) + ck_ref[...]

    fq = _dot(hb, win_ref[:, O_FQ:O_FQ + W_FQ])
    fk = _dot(hb, win_ref[:, O_FK:O_FK + W_FK])
    scale_f = FOX_HD ** -0.5 * LOG2E
    for h in range(FOX_HEADS):
        sl = slice(h * HP, (h + 1) * HP)
        qh = _rms(fq[:, sl], gfq_ref[...], FOX_HD) * scale_f
        qa_ref[:, off + h * HP:off + (h + 1) * HP] = (qh + aug_q[:, sl]).astype(BF16)
        kh = _rms(fk[:, sl], gfk_ref[...], FOX_HD)
        fkp_ref[:, sl] = kh
        ka_ref[:, off + h * HP:off + (h + 1) * HP] = (kh + aug_k[:, sl]).astype(BF16)
    fvp = _dot(hb, win_ref[:, O_FV:O_FV + W_FV])
    fvp_ref[...] = fvp
    for h in range(FOX_HEADS):
        sl = slice(h * HP, (h + 1) * HP)
        va_ref[:, off + h * HP:off + (h + 1) * HP] = (fvp[:, sl] + one_ref[...]).astype(BF16)


def _pad_heads(w, n_heads, width):
    k = w.shape[0]
    w = w.reshape(k, n_heads, width)
    return jnp.pad(w, ((0, 0), (0, 0), (0, HP - width))).reshape(k, n_heads * HP)


def _feat_weights(w_in, w_uq, w_ukv, g_mla_qn, g_mla_kn, g_fox_qn, g_fox_kn, b_forget):
    d = w_in.shape[0]
    o = np.cumsum([0, Q_LORA, KV_LORA, MLA_ROPE, FOX_HEADS * FOX_HD, FOX_HEADS * FOX_HD,
                   FOX_HEADS * FOX_HD, FOX_HEADS])
    w_cq, w_ckv, w_kpe, w_fq, w_fk, w_fv, w_fl = [w_in[:, o[i]:o[i + 1]] for i in range(7)]
    kpe_grp = jnp.pad(w_kpe, ((0, 0), (MLA_NOPE, HP - MLA_QK)))
    misc = jnp.concatenate([w_kpe, w_fl, w_fl, w_fl], axis=1)
    misc = jnp.pad(misc, ((0, 0), (0, W_MISC - misc.shape[1])))
    win_p = jnp.concatenate([w_cq, w_ckv, jnp.tile(kpe_grp, (1, MLA_HEADS)),
                             _pad_heads(w_fq, FOX_HEADS, FOX_HD), _pad_heads(w_fk, FOX_HEADS, FOX_HD),
                             _pad_heads(w_fv, FOX_HEADS, FOX_HD), misc], axis=1).astype(BF16)
    assert win_p.shape == (d, W_IN_P)
    wuq_p = _pad_heads(w_uq, MLA_HEADS, MLA_QK).astype(BF16)
    kvw = w_ukv.reshape(KV_LORA, MLA_HEADS, MLA_NOPE + MLA_V)
    wk = kvw[:, :, :MLA_NOPE].reshape(KV_LORA, MLA_HEADS * MLA_NOPE)
    wv = kvw[:, :, MLA_NOPE:].reshape(KV_LORA, MLA_HEADS * MLA_V)
    wukv_p = jnp.concatenate([_pad_heads(wk, MLA_HEADS, MLA_NOPE), _pad_heads(wv, MLA_HEADS, MLA_V)],
                             axis=1).astype(BF16)
    pad1 = lambda g, n: jnp.pad(g, (0, HP - n)).reshape(1, HP)
    b3 = jnp.pad(jnp.tile(b_forget, 3), (MISC_LF, W_MISC - MISC_LF - 3 * FOX_HEADS)).reshape(1, W_MISC)
    return (win_p, wuq_p, wukv_p, wk.astype(BF16), wv.astype(BF16), pad1(g_mla_qn, MLA_QK),
            pad1(g_mla_kn, MLA_QK), pad1(g_fox_qn, FOX_HD), pad1(g_fox_kn, FOX_HD), b3)


def _aug_constants():
    pq = np.zeros((W_MISC, FOX_HEADS * HP), np.float32)
    pk = np.zeros((W_MISC, FOX_HEADS * HP), np.float32)
    cq = np.zeros((1, FOX_HEADS * HP), np.float32)
    ck = np.zeros((1, FOX_HEADS * HP), np.float32)
    for h in range(FOX_HEADS):
        for j in range(3):
            pq[MISC_LF + j * FOX_HEADS + h, h * HP + FOX_HD + j] = 1.0
            ck[0, h * HP + FOX_HD + j] = 1.0
            pk[MISC_LF + j * FOX_HEADS + h, h * HP + FOX_HD + 3 + j] = -1.0
            cq[0, h * HP + FOX_HD + 3 + j] = 1.0
    return jnp.asarray(pq, BF16), jnp.asarray(pk, BF16), jnp.asarray(cq), jnp.asarray(ck)


def _rope_tables(pos):
    half = MLA_ROPE // 2
    inv = jnp.power(ROPE_THETA, -jnp.arange(half, dtype=F32) / half)
    ang = pos.astype(F32)[:, None] * inv[None, :]
    return jnp.cos(ang), jnp.sin(ang)


def _feat_call(rows, pos, tm, g_norm1, g_q_lora, g_kv_lora, fw):
    win_p, wuq_p, wukv_p, _, _, gmq, gmk, gfq, gfk, b3 = fw
    r, d = rows.shape
    cos, sin = _rope_tables(pos)
    n = cos.shape[0]
    one, zero = jnp.ones((n, MLA_NOPE), F32), jnp.zeros((n, HP - MLA_QK), F32)
    z16 = jnp.zeros_like(sin)
    cosp = jnp.concatenate([one, cos, cos, one[:, :HP - MLA_QK]], axis=1)
    sina = jnp.concatenate([zero, zero, -sin, z16, zero], axis=1)
    sinb = jnp.concatenate([zero, zero, z16, sin, zero], axis=1)
    ltri = jnp.asarray(np.tril(np.ones((tm, tm), np.float32)), BF16)
    pq, pk, cq, ck = _aug_constants()
    one = jnp.zeros((1, HP), F32).at[0, MLA_V].set(1.0)
    row = lambda w: pl.BlockSpec((tm, w), lambda i: (i, 0))
    full = lambda a: pl.BlockSpec(a.shape, lambda i: (0,) * a.ndim)
    g1 = g_norm1.reshape(1, d)
    gq = g_q_lora.reshape(1, Q_LORA)
    gkv = g_kv_lora.reshape(1, KV_LORA)
    consts = (g1, win_p, gq, gkv, wuq_p, wukv_p, gmq, gmk, gfq, gfk, b3)
    tail = (ltri, pq, pk, cq, ck, one)
    nh = MLA_HEADS + FOX_HEADS
    out_shape = (
        jax.ShapeDtypeStruct((r, KV_LORA), F32), jax.ShapeDtypeStruct((r, MLA_ROPE), F32),
        jax.ShapeDtypeStruct((r, FOX_HEADS * HP), F32), jax.ShapeDtypeStruct((r, FOX_HEADS * HP), F32),
        jax.ShapeDtypeStruct((r, FOX_HEADS), F32), jax.ShapeDtypeStruct((r, nh * HP), BF16),
        jax.ShapeDtypeStruct((r, nh * HP), BF16), jax.ShapeDtypeStruct((r, nh * HP), BF16))
    return pl.pallas_call(
        _feat_kernel,
        grid=(r // tm,),
        in_specs=[row(d)] + [full(a) for a in consts] + [row(HP), row(HP), row(HP)] + [full(a) for a in tail],
        out_specs=[row(s.shape[1]) for s in out_shape],
        out_shape=out_shape,
        scratch_shapes=[pltpu.VMEM((1, W_MISC), F32)],
        compiler_params=pltpu.CompilerParams(dimension_semantics=("arbitrary",),
                                             vmem_limit_bytes=VMEM_LIMIT),
        name="feat",
    )(rows, *consts, cosp, sina, sinb, *tail)


ATTN_UNROLL = 4


def _attn_kernel(q_ref, k_ref, v_ref, o_ref, m_ref, acc_ref, *, tq, tk):
    assert tq == tk
    qi = pl.program_id(1)
    qs = [q_ref[:, j * HP:(j + 1) * HP] for j in range(2)]

    def block(kb, carry, masked):
        start = pl.multiple_of(kb * tk, tk)
        new = []
        for j in range(2):
            m, acc = carry[2 * j], carry[2 * j + 1]
            k = k_ref[pl.ds(start, tk), j * HP:(j + 1) * HP]
            v = v_ref[pl.ds(start, tk), j * HP:(j + 1) * HP]
            s = _dot_nt(qs[j], k)
            if masked:
                rowi = qi * tq + lax.broadcasted_iota(jnp.int32, s.shape, 0)
                coli = kb * tk + lax.broadcasted_iota(jnp.int32, s.shape, 1)
                s = jnp.where(coli <= rowi, s, NEG_INF)
            m_new = jnp.maximum(m, jnp.max(s, axis=-1, keepdims=True))
            alpha = jnp.exp2(m - m_new)
            p = jnp.exp2((s - m_new).astype(BF16))
            new += [m_new, alpha * acc + _dot(p, v)]
        return tuple(new)

    def save(carry):
        for j in range(2):
            m_ref[j] = carry[2 * j]
            acc_ref[j] = carry[2 * j + 1]

    def load():
        return (m_ref[0], acc_ref[0], m_ref[1], acc_ref[1])

    def trip(i, c):
        for u in range(ATTN_UNROLL):
            c = block(ATTN_UNROLL * i + u, c, False)
        return c

    init = (jnp.full((tq, 1), NEG_INF, F32), jnp.zeros((tq, HP), F32)) * 2
    save(lax.fori_loop(0, qi // ATTN_UNROLL, trip, init))
    for u in range(ATTN_UNROLL - 1):
        @pl.when(qi % ATTN_UNROLL > u)
        def _():
            save(block((qi // ATTN_UNROLL) * ATTN_UNROLL + u, load(), False))

    carry = block(qi, load(), True)
    outs = [carry[2 * j + 1] / carry[2 * j + 1][:, MLA_V:MLA_V + 1] for j in range(2)]
    lane = lax.broadcasted_iota(jnp.int32, outs[0].shape, 1)
    o_ref[...] = jnp.where(lane < MLA_V, outs[0], pltpu.roll(outs[1], MLA_V, 1))


def _attn_call(qa, ka, va, tp, tq, tk):
    npairs = qa.shape[1] // (2 * HP)
    return pl.pallas_call(
        functools.partial(_attn_kernel, tq=tq, tk=tk),
        grid=(npairs, tp // tq),
        in_specs=[pl.BlockSpec((tq, 2 * HP), lambda p, i: (i, p)),
                  pl.BlockSpec((tp, 2 * HP), lambda p, i: (0, p)),
                  pl.BlockSpec((tp, 2 * HP), lambda p, i: (0, p))],
        out_specs=pl.BlockSpec((tq, 2 * MLA_V), lambda p, i: (i, p)),
        out_shape=jax.ShapeDtypeStruct((tp, npairs * 2 * MLA_V), F32),
        scratch_shapes=[pltpu.VMEM((2, tq, 1), F32), pltpu.VMEM((2, tq, HP), F32)],
        compiler_params=pltpu.CompilerParams(dimension_semantics=("arbitrary", "arbitrary"),
                                             vmem_limit_bytes=VMEM_LIMIT),
        name="attn",
    )(qa, ka, va)


def _softmax_step(s, pv, m_ref, l_ref, acc_ref):
    m_old = m_ref[...]
    m_new = jnp.maximum(m_old, jnp.max(s, axis=-1, keepdims=True))
    alpha = jnp.exp2(m_old - m_new)
    p = jnp.exp2(s - m_new)
    l_ref[...] = alpha * l_ref[...] + jnp.sum(p, axis=-1, keepdims=True)
    acc_ref[...] = alpha * acc_ref[...] + pv(p.astype(BF16))
    m_ref[...] = m_new


def _head_diag(x, hm):
    ns = x.shape[0] // MLA_HEADS
    return jnp.sum(x.reshape(ns, MLA_HEADS, x.shape[1]) * hm[None], axis=1)


def _decode_kernel(pt_ref, *refs, pg):
    pools = refs[:5]
    (cos_ref, sin_ref, qbm_ref, qbf_ref, qn_ref, qr_ref, qf_ref, kan_ref, latn_ref, fvn_ref, lfn_ref,
     wk_ref, wv_ref, gn_ref, gr_ref, bexp_ref, lstrict_ref,
     ltri8_ref, es_ref, fh_ref, hm_ref, causal_ref,
     o_ref, mm_ref, lm_ref, am_ref, mf_ref, lfs_ref, af_ref, carry_ref, cnq_ref, qng_ref,
     lat_buf, rope_buf, fk_buf, fv_buf, lf_buf, sems) = refs[5:]
    bufs = (lat_buf, rope_buf, fk_buf, fv_buf, lf_buf)
    b, g = pl.program_id(0), pl.program_id(1)
    nch = pl.num_programs(1)
    step = b * nch + g
    slot = step % 2
    nm = MLA_HEADS * HP

    def page_copies(bb, gg, sl):
        first = bb * (nch * pg) + (nch - 1 - gg) * pg
        out = []
        for i in range(pg):
            pid = pt_ref[first + i]
            tok = pl.ds(i * LANES, LANES)
            dsts = (lat_buf.at[sl, tok, :], rope_buf.at[sl, :, tok], fk_buf.at[sl, :, tok],
                    fv_buf.at[sl, :, tok], lf_buf.at[sl, pl.ds(i * FOX_HEADS, FOX_HEADS), :])
            out += [pltpu.make_async_copy(pool.at[pid], dst, sems.at[sl, j])
                    for j, (pool, dst) in enumerate(zip(pools, dsts))]
        return out

    @pl.when(step == 0)
    def _():
        for cp in page_copies(b, g, slot):
            cp.start()

    @pl.when(step + 1 < pl.num_programs(0) * nch)
    def _():
        wrap = g + 1 == nch
        for cp in page_copies(jnp.where(wrap, b + 1, b), jnp.where(wrap, 0, g + 1), 1 - slot):
            cp.start()

    @pl.when(g == 0)
    def _():
        causal = causal_ref[...]
        ka = kan_ref[0]
        s_m = jnp.where(causal > 0, _dot_nt(qbm_ref[0], ka[:, :nm]), NEG_INF)
        mm_ref[...] = jnp.full_like(mm_ref, NEG_INF)
        lm_ref[...] = jnp.zeros_like(lm_ref)
        am_ref[...] = jnp.zeros_like(am_ref)
        lat_n = latn_ref[0].astype(BF16)
        _softmax_step(s_m, lambda p: _dot(p, lat_n), mm_ref, lm_ref, am_ref)
        cn = _dot_exact_rhs(ltri8_ref[...], lfn_ref[0])
        cnq = jnp.sum(_dot_exact_rhs(es_ref[...], cn) * fh_ref[...].astype(F32), axis=-1, keepdims=True)
        a, b, c = _split3(cn)
        fh = fh_ref[...]
        cnk = _dot_nt(fh, a) + _dot_nt(fh, b) + _dot_nt(fh, c)
        s_f = _dot_nt(qbf_ref[0], ka[:, nm:]) + (cnq - cnk) * LOG2E
        s_f = jnp.where(causal > 0, s_f, NEG_INF)
        mf_ref[...] = jnp.full_like(mf_ref, NEG_INF)
        lfs_ref[...] = jnp.zeros_like(lfs_ref)
        af_ref[...] = jnp.zeros_like(af_ref)
        fv_n = fvn_ref[0].astype(BF16)
        _softmax_step(s_f, lambda p: _dot(p, fv_n), mf_ref, lfs_ref, af_ref)
        cnq_ref[...] = cnq * LOG2E
        carry_ref[...] = jnp.zeros_like(carry_ref)
        qng_ref[...] = (qn_ref[0].astype(F32) * gn_ref[...]).astype(BF16)

    for cp in page_copies(b, g, slot):
        cp.wait()

    half = MLA_ROPE // 2
    nrep = qf_ref.shape[1] // FOX_HEADS
    lat = lat_buf[slot].astype(BF16)
    kn = _dot(lat, wk_ref[...]).astype(BF16)
    kpe = rope_buf[slot]
    ssq = _dot_nt(bexp_ref[...], kn * kn) + jnp.sum(kpe * kpe, axis=0, keepdims=True)
    r_t = lax.rsqrt(ssq * (1.0 / MLA_QK) + EPS)
    kg = kpe * gr_ref[...]
    sw = jnp.concatenate([kg[half:], kg[:half]], axis=0)
    kr = kg * cos_ref[...] + sw * sin_ref[...]
    s_m = (_dot_nt(qng_ref[...], kn) + _dot(qr_ref[0], kr.astype(BF16))) * r_t
    _softmax_step(s_m, lambda p: _dot(p, lat), mm_ref, lm_ref, am_ref)
    lf = lf_buf[slot]
    a, b, c = _split3(lf)
    lstrict = lstrict_ref[...]
    within = _dot(a, lstrict) + _dot(b, lstrict) + _dot(c, lstrict)
    tot = jnp.sum(lf, axis=-1, keepdims=True)
    run = carry_ref[...]
    decs = [None] * pg
    for i in reversed(range(pg)):
        rows = slice(i * FOX_HEADS, (i + 1) * FOX_HEADS)
        decs[i] = jnp.concatenate([within[rows] + run] * nrep, axis=0)
        run = run + tot[rows]
    carry_ref[...] = run
    dec = jnp.concatenate(decs, axis=1) * LOG2E + cnq_ref[...]
    s_f = _dot(qf_ref[0], fk_buf[slot].astype(BF16)) + dec
    fv = fv_buf[slot].astype(BF16)
    _softmax_step(s_f, lambda p: _dot_nt(p, fv), mf_ref, lfs_ref, af_ref)

    @pl.when(g == pl.num_programs(1) - 1)
    def _():
        hm = hm_ref[...]
        nv = MLA_HEADS * MLA_V
        o_lat = (am_ref[...] / lm_ref[...]).astype(BF16)
        o_ref[0, :, 0:nv] = _head_diag(_dot(o_lat, wv_ref[...]), hm)
        o_ref[0, :, nv:2 * nv] = _head_diag(af_ref[...] / lfs_ref[...], hm)


def _decode_constants(ns):
    nr = ns * MLA_HEADS
    hh = np.arange(nr) % MLA_HEADS
    ss = np.arange(nr) // MLA_HEADS
    bexp = (np.arange(MLA_HEADS * MLA_NOPE)[None, :] // MLA_NOPE == hh[:, None]).astype(np.float32)
    lstrict = np.tril(np.ones((LANES, LANES), np.float32), -1)
    ltri8 = np.tril(np.ones((ns, ns), np.float32))
    es = (np.arange(ns)[None, :] == ss[:, None]).astype(np.float32)
    fh = (np.arange(FOX_HEADS)[None, :] == hh[:, None]).astype(np.float32)
    hm = (np.arange(MLA_HEADS * MLA_V)[None, :] // MLA_V == np.arange(MLA_HEADS)[:, None]).astype(np.float32)
    causal = (np.arange(ns)[None, :] <= ss[:, None]).astype(np.float32)
    bf = lambda a: jnp.asarray(a, BF16)
    return (bf(bexp), bf(lstrict), bf(ltri8), bf(es), bf(fh), jnp.asarray(hm), jnp.asarray(causal))


def _block_diag_rows(q, width):
    b, s, h, w = q.shape
    eye = jnp.eye(h, dtype=q.dtype)
    return (q[:, :, :, None, :] * eye[None, None, :, :, None]).reshape(b, s * h, h * w)


def _decode_call(qa_s, ka_s, lat_s, fv_s, lf_s, pools, page_table, fw, g_mla_kn, pg):
    lat_pool, rope_pool, fk_pool, fv_pool, lf_pool = pools
    nb, npages = page_table.shape
    ns = qa_s.shape[0] // nb
    nr = ns * MLA_HEADS
    nch = npages // pg
    page = lat_pool.shape[1]
    assert page == LANES and npages % pg == 0
    rope_pool = jnp.swapaxes(rope_pool, 1, 2)
    fk_pool = jnp.transpose(fk_pool, (0, 2, 3, 1)).reshape(fk_pool.shape[0], -1, page)
    fv_pool = jnp.transpose(fv_pool, (0, 2, 3, 1)).reshape(fv_pool.shape[0], -1, page)
    lf_pool = jnp.swapaxes(lf_pool, 1, 2)
    past = npages * page
    nm = MLA_HEADS * HP
    _, _, _, wk, wv, _, _, _, _, _ = fw
    q4 = qa_s.reshape(nb, ns, MLA_HEADS + FOX_HEADS, HP)
    qm, qfx = q4[:, :, :MLA_HEADS], q4[:, :, MLA_HEADS:, :FOX_HD]
    qbm = _block_diag_rows(qm, HP)
    qbf = _block_diag_rows(jnp.pad(qfx, ((0, 0), (0, 0), (0, 0), (0, HP - FOX_HD))), HP)
    qn = _block_diag_rows(qm[..., :MLA_NOPE], MLA_NOPE)
    qr = qm[..., MLA_NOPE:MLA_QK].reshape(nb, nr, MLA_ROPE)
    qf = _block_diag_rows(qfx, FOX_HD)
    kan = ka_s.reshape(nb, ns, ka_s.shape[1])
    latn = lat_s.reshape(nb, ns, lat_s.shape[1])
    fvn = fv_s.reshape(nb, ns, -1)
    lfn = lf_s.reshape(nb, ns, FOX_HEADS)
    cos, sin = _rope_tables(jnp.arange(past, dtype=jnp.int32))
    cos2 = jnp.concatenate([cos, cos], axis=1).T
    sin2 = jnp.concatenate([-sin, sin], axis=1).T
    gn = jnp.tile(g_mla_kn[:MLA_NOPE], MLA_HEADS).reshape(1, MLA_HEADS * MLA_NOPE)
    gr = jnp.broadcast_to(g_mla_kn[MLA_NOPE:MLA_QK].reshape(MLA_ROPE, 1), (MLA_ROPE, pg * page))
    consts = (wk, wv, gn, gr) + _decode_constants(ns)
    pt_flat = page_table.reshape(-1).astype(jnp.int32)

    page_args = (lat_pool, rope_pool, fk_pool, fv_pool, lf_pool)
    page_specs = [pl.BlockSpec(memory_space=pl.ANY)] * len(page_args)
    t = pg * page
    page_bufs = [pltpu.VMEM((2, t, lat_pool.shape[2]), F32), pltpu.VMEM((2, rope_pool.shape[1], t), F32),
                 pltpu.VMEM((2, fk_pool.shape[1], t), F32), pltpu.VMEM((2, fv_pool.shape[1], t), F32),
                 pltpu.VMEM((2, pg * lf_pool.shape[1], page), F32),
                 pltpu.SemaphoreType.DMA((2, len(page_args)))]
    tab = pl.BlockSpec((MLA_ROPE, pg * page), lambda b, g, pt: (0, nch - 1 - g))
    seq = lambda a: pl.BlockSpec((1,) + a.shape[1:], lambda b, g, pt: (b,) + (0,) * (a.ndim - 1))
    full = lambda a: pl.BlockSpec(a.shape, lambda b, g, pt: (0,) * a.ndim)
    per_seq = (qbm, qbf, qn, qr, qf, kan, latn, fvn, lfn)
    nv = MLA_HEADS * MLA_V
    grid_spec = pltpu.PrefetchScalarGridSpec(
        num_scalar_prefetch=1,
        grid=(nb, nch),
        in_specs=page_specs + [tab, tab] + [seq(a) for a in per_seq] + [full(a) for a in consts],
        out_specs=pl.BlockSpec((1, ns, 2 * nv), lambda b, g, pt: (b, 0, 0)),
        scratch_shapes=[pltpu.VMEM((nr, 1), F32), pltpu.VMEM((nr, 1), F32), pltpu.VMEM((nr, KV_LORA), F32),
                        pltpu.VMEM((nr, 1), F32), pltpu.VMEM((nr, 1), F32), pltpu.VMEM((nr, nv), F32),
                        pltpu.VMEM((FOX_HEADS, 1), F32), pltpu.VMEM((nr, 1), F32),
                        pltpu.VMEM((nr, MLA_HEADS * MLA_NOPE), BF16)] + page_bufs)
    out = pl.pallas_call(
        functools.partial(_decode_kernel, pg=pg),
        grid_spec=grid_spec,
        out_shape=jax.ShapeDtypeStruct((nb, ns, 2 * nv), F32),
        compiler_params=pltpu.CompilerParams(dimension_semantics=("arbitrary", "arbitrary"),
                                             vmem_limit_bytes=VMEM_LIMIT),
        name="decode",
    )(pt_flat, *page_args, cos2, sin2, *per_seq, *consts)
    return out.reshape(nb * ns, 2 * nv)


def _merge_kernel(x_ref, o_ref, gmo_ref, gfo_ref, wout_ref, g2_ref, wpq_ref, ka_ref, kb_ref,
                  x1_ref, h2_ref, sa_ref, sb_ref):
    o = o_ref[...]
    nv = MLA_HEADS * MLA_V
    mixed = jnp.concatenate([_rms(o[:, :nv], gmo_ref[...], nv).astype(BF16),
                             _rms(o[:, nv:], gfo_ref[...], o.shape[1] - nv).astype(BF16)], axis=1)
    x1 = x_ref[...] + _dot(mixed, wout_ref[...])
    x1_ref[...] = x1
    h2 = _rms(x1, g2_ref[...], x1.shape[1]).astype(BF16)
    h2_ref[...] = h2
    pq = _dot(h2, wpq_ref[...])
    half = PEER_DK // 2
    for h in range(PEER_HEADS):
        qa = pq[:, h * PEER_DK:h * PEER_DK + half].astype(BF16)
        qb = pq[:, h * PEER_DK + half:(h + 1) * PEER_DK].astype(BF16)
        sa_ref[h * PEER_NKEYS:(h + 1) * PEER_NKEYS, :] = _dot_nt(ka_ref[h], qa)
        sb_ref[h * PEER_NKEYS:(h + 1) * PEER_NKEYS, :] = _dot_nt(kb_ref[h], qb)


def _merge_call(x, o, tm, g_mla_out, g_fox_out, w_out, g_norm2, w_peer_query, keys_a, keys_b):
    r, d = x.shape
    row = lambda w: pl.BlockSpec((tm, w), lambda i: (i, 0))
    full = lambda a: pl.BlockSpec(a.shape, lambda i: (0,) * a.ndim)
    col = pl.BlockSpec((PEER_HEADS * PEER_NKEYS, tm), lambda i: (0, i))
    consts = (g_mla_out.reshape(1, -1), g_fox_out.reshape(1, -1), w_out.astype(BF16),
              g_norm2.reshape(1, -1), w_peer_query.astype(BF16), keys_a.astype(BF16), keys_b.astype(BF16))
    nk = PEER_HEADS * PEER_NKEYS
    return pl.pallas_call(
        _merge_kernel,
        grid=(r // tm,),
        in_specs=[row(d), row(o.shape[1])] + [full(a) for a in consts],
        out_specs=[row(d), row(d), col, col],
        out_shape=(jax.ShapeDtypeStruct((r, d), F32), jax.ShapeDtypeStruct((r, d), BF16),
                   jax.ShapeDtypeStruct((nk, r), F32), jax.ShapeDtypeStruct((nk, r), F32)),
        compiler_params=pltpu.CompilerParams(dimension_semantics=("arbitrary",),
                                             vmem_limit_bytes=VMEM_LIMIT),
        name="merge",
    )(x, o, *consts)


_CAND = [(i, j) for i in range(PEER_TOPK) for j in range(PEER_TOPK) if (i + 1) * (j + 1) <= PEER_TOPK]
_NCAND = -(-len(_CAND) // 8) * 8


def _extract_max(cur, order, first_only):
    m = jnp.max(cur, axis=0, keepdims=True)
    if not first_only:
        return m, cur == m
    first = jnp.min(jnp.where(cur == m, order, 1e9), axis=0, keepdims=True)
    return m, order == first


def _topk_tables(sa, sb, ci, cj, first_only):
    k = PEER_TOPK
    tn = sa.shape[1]
    key_order = lax.broadcasted_iota(jnp.int32, (PEER_NKEYS, tn), 0).astype(F32)
    cand_order = lax.broadcasted_iota(jnp.int32, (_NCAND, tn), 0).astype(F32)
    ranks, vals = [], []
    for s in (sa, sb):
        cur = s
        rank = jnp.full(s.shape, float(k), F32)
        v = []
        for i in range(k):
            m, hot = _extract_max(cur, key_order, first_only)
            v.append(m)
            rank = jnp.where(hot, float(i), rank)
            cur = jnp.where(hot, -jnp.inf, cur)
        ranks.append(rank)
        vals.append(v)
    rank_a, rank_b = ranks
    va, vb = vals
    cand = jnp.full((_NCAND, tn), -jnp.inf, F32)
    for i in range(k):
        cand = jnp.where(ci == float(i), va[i], cand)
    cur = jnp.full((_NCAND, tn), -jnp.inf, F32)
    for j in range(k):
        cur = jnp.where(cj == float(j), cand + vb[j], cur)
    top = va[0] + vb[0]
    z = jnp.zeros((1, tn), F32)
    sel = jnp.zeros((_NCAND, tn), F32)
    for _ in range(k):
        m, hot = _extract_max(cur, cand_order, first_only)
        z = z + jnp.exp(m - top)
        sel = jnp.where(hot, 1.0, sel)
        cur = jnp.where(hot, -jnp.inf, cur)
    ca = jnp.zeros(sa.shape, F32)
    for i in range(k):
        c_i = jnp.sum(jnp.where(ci == float(i), sel, 0.0), axis=0, keepdims=True)
        ca = jnp.where(rank_a == float(i), c_i, ca)
    count = lambda hit: jnp.sum(jnp.where(hit, 1.0, 0.0), axis=0, keepdims=True)
    taken = jnp.maximum(jnp.maximum(count(rank_a < k), count(rank_b < k)), count(sel > 0))
    return rank_b, ca, jnp.exp(sa - va[0]) / z, jnp.exp(sb - vb[0]), taken


def _topk_kernel(sa_ref, sb_ref, ci_ref, cj_ref, rb_ref, ca_ref, ea_ref, eb_ref):
    def run(first_only):
        rank_b, ca, ea, eb, taken = _topk_tables(sa_ref[...], sb_ref[...], ci_ref[...], cj_ref[...],
                                                 first_only)
        rb_ref[...] = rank_b.astype(rb_ref.dtype)
        ca_ref[...] = ca
        ea_ref[...] = ea
        eb_ref[...] = eb.astype(eb_ref.dtype)
        return taken

    taken = run(False)

    @pl.when(jnp.max(taken) > PEER_TOPK)
    def _():
        run(True)


def _topk_call(sa, sb, tn):
    nk, r = sa.shape
    ci = np.full((_NCAND, tn), -1.0, np.float32)
    cj = np.full((_NCAND, tn), -1.0, np.float32)
    for n, (i, j) in enumerate(_CAND):
        ci[n, :], cj[n, :] = i, j
    col = pl.BlockSpec((PEER_NKEYS, tn), lambda i, h: (h, i))
    full = pl.BlockSpec((_NCAND, tn), lambda i, h: (0, 0))
    return pl.pallas_call(
        _topk_kernel,
        grid=(r // tn, nk // PEER_NKEYS),
        in_specs=[col, col, full, full],
        out_specs=[col] * 4,
        out_shape=tuple(jax.ShapeDtypeStruct((nk, r), dt) for dt in (BF16, F32, F32, BF16)),
        compiler_params=pltpu.CompilerParams(dimension_semantics=("arbitrary", "arbitrary"),
                                             vmem_limit_bytes=VMEM_LIMIT),
        name="topk",
    )(sa, sb, jnp.asarray(ci), jnp.asarray(cj))


def _peer_kernel(h2_ref, x1_ref, u_ref, vt_ref, rb_ref, ca_ref, ea_ref, eb_ref, y_ref, acc_ref, w_ref,
                 *, a_per_step):
    c = pl.program_id(1)

    @pl.when(c == 0)
    def _():
        acc_ref[...] = jnp.zeros_like(acc_ref)

    s = _dot_nt(u_ref[...], h2_ref[...])
    act = 0.5 * s * (1.0 + lax.erf(s * np.float32(np.sqrt(0.5))))
    zero = jnp.zeros((), BF16)
    for aa in range(a_per_step):
        gate = jnp.zeros((PEER_NKEYS, s.shape[1]), BF16)
        for h in range(PEER_HEADS):
            row = pl.ds(h * PEER_NKEYS + c * a_per_step + aa, 1)
            sl = slice(h * PEER_NKEYS, (h + 1) * PEER_NKEYS)
            c_a = ca_ref[row, :].astype(BF16)
            e_a = ea_ref[row, :].astype(BF16)
            gate = gate + jnp.where(rb_ref[sl, :] < c_a, eb_ref[sl, :], zero) * e_a
        blk = slice(aa * PEER_NKEYS, (aa + 1) * PEER_NKEYS)
        w_ref[blk, :] = gate * act[blk, :].astype(BF16)
    acc_ref[...] += _dot(vt_ref[...], w_ref[...])

    @pl.when(c == pl.num_programs(1) - 1)
    def _():
        y_ref[...] = x1_ref[...] + acc_ref[...].T


def _peer_call(h2, x1, u, vt, tables, tm, a_per_step):
    r, d = h2.shape
    ne = u.shape[0]
    ec = a_per_step * PEER_NKEYS
    nk = PEER_HEADS * PEER_NKEYS
    row = pl.BlockSpec((tm, d), lambda i, c: (i, 0))
    col = pl.BlockSpec((nk, tm), lambda i, c: (0, i))
    return pl.pallas_call(
        functools.partial(_peer_kernel, a_per_step=a_per_step),
        grid=(r // tm, ne // ec),
        in_specs=[row, row, pl.BlockSpec((ec, d), lambda i, c: (c, 0)),
                  pl.BlockSpec((d, ec), lambda i, c: (0, c)), col, col, col, col],
        out_specs=row,
        out_shape=jax.ShapeDtypeStruct((r, d), F32),
        scratch_shapes=[pltpu.VMEM((d, tm), F32), pltpu.VMEM((ec, tm), BF16)],
        compiler_params=pltpu.CompilerParams(dimension_semantics=("arbitrary", "arbitrary"),
                                             vmem_limit_bytes=VMEM_LIMIT),
        name="peer",
    )(h2, x1, u, vt, *tables)


def _round_up(n, m):
    return -(-n // m) * m


def _layer(xp, xs, pools, page_table, p, tiles):
    t, d = xp.shape
    nb, ns, _ = xs.shape
    past = page_table.shape[1] * pools[0].shape[1]
    tp = _round_up(t, max(tiles["tq"], tiles["tp"]))
    n_s = nb * ns
    r = tp + n_s
    assert r % tiles["tm"] == 0 and r % tiles["tp"] == 0
    rows = jnp.concatenate([xp, jnp.zeros((tp - t, d), xp.dtype), xs.reshape(n_s, d)], axis=0)
    pos = jnp.concatenate([jnp.arange(tp, dtype=jnp.int32),
                           past + (jnp.arange(n_s, dtype=jnp.int32) % ns)])
    fw = _feat_weights(p['w_in'], p['w_uq'], p['w_ukv'], p['g_mla_qn'], p['g_mla_kn'],
                       p['g_fox_qn'], p['g_fox_kn'], p['b_forget'])
    lat, kpe, fkp, fvp, lf, qa, ka, va = _feat_call(rows, pos, tiles["tm"], p['g_norm1'], p['g_q_lora'],
                                                    p['g_kv_lora'], fw)
    fk = fkp.reshape(r, FOX_HEADS, HP)[:, :, :FOX_HD]
    fvh = fvp.reshape(r, FOX_HEADS, HP)[:, :, :FOX_HD]
    o_p = _attn_call(qa, ka, va, tp, tiles["tq"], tiles["tk"])
    o_s = _decode_call(qa[tp:], ka[tp:], lat[tp:], fvh[tp:], lf[tp:], pools, page_table, fw, p['g_mla_kn'],
                       tiles["pg"])
    o = jnp.concatenate([o_p, o_s], axis=0)
    x1, h2, sa, sb = _merge_call(rows, o, tiles["tm"], p['g_mla_out'], p['g_fox_out'], p['w_out'],
                                 p['g_norm2'], p['w_peer_query'], p['peer_keys_a'], p['peer_keys_b'])
    tables = _topk_call(sa, sb, tiles["tn"])
    y = _peer_call(h2, x1, p['peer_u'].astype(BF16), p['peer_v'].astype(BF16).T, tables,
                   tiles["tp"], tiles["ap"])
    prm = lambda a: a[:t][None]
    smp = lambda a: a[tp:].reshape((nb, ns) + a.shape[1:])
    outs = (lat, kpe, fk, fvh, lf)
    return (y[:t], y[tp:].reshape(nb, ns, d), tuple(prm(a) for a in outs), tuple(smp(a) for a in outs))


_TILES = dict(tm=256, tq=512, tk=512, pg=16, tn=256, tp=512, ap=8)


def kernel(x_prompt, x_sample, cache_mla_latent, cache_mla_rope, cache_fox_k, cache_fox_v, cache_fox_logf,
           page_table, meta_tokens, g_norm1, w_in, g_q_lora, g_kv_lora, w_uq, w_ukv, g_mla_qn, g_mla_kn,
           g_fox_qn, g_fox_kn, b_forget, g_mla_out, g_fox_out, w_out, g_norm2, w_peer_query, peer_keys_a,
           peer_keys_b, peer_u, peer_v):
    tiles = _TILES
    depth = w_in.shape[0]
    assert depth == 1 and x_prompt.shape[0] == 1
    xp = jnp.concatenate([meta_tokens.astype(x_prompt.dtype), x_prompt[0]], axis=0)
    xs = x_sample
    p_rows, s_rows = [], []
    for l in range(depth):
        p = {
            'g_norm1': g_norm1[l], 'w_in': w_in[l], 'g_q_lora': g_q_lora[l], 'g_kv_lora': g_kv_lora[l],
            'w_uq': w_uq[l], 'w_ukv': w_ukv[l], 'g_mla_qn': g_mla_qn[l], 'g_mla_kn': g_mla_kn[l],
            'g_fox_qn': g_fox_qn[l], 'g_fox_kn': g_fox_kn[l], 'b_forget': b_forget[l],
            'g_mla_out': g_mla_out[l], 'g_fox_out': g_fox_out[l], 'w_out': w_out[l],
            'g_norm2': g_norm2[l], 'w_peer_query': w_peer_query[l], 'peer_keys_a': peer_keys_a[l],
            'peer_keys_b': peer_keys_b[l], 'peer_u': peer_u[l], 'peer_v': peer_v[l],
        }
        pools = (cache_mla_latent[l], cache_mla_rope[l], cache_fox_k[l], cache_fox_v[l], cache_fox_logf[l])
        yp, ys, rp, rs = _layer(xp, xs, pools, page_table, p, tiles)
        xp, xs = yp, ys
        p_rows.append(rp)
        s_rows.append(rs)
    p_lat, p_rope, p_fk, p_fv, p_lf = [jnp.stack(r) for r in zip(*p_rows)]
    s_lat, s_rope, s_fk, s_fv, s_lf = [jnp.stack(r) for r in zip(*s_rows)]
    return (xp[NUM_META:][None], xs, p_lat, p_rope, p_fk, p_fv, p_lf, s_lat, s_rope, s_fk, s_fv, s_lf)
```
